```python
import math
import jax, jax.numpy as jnp
from jax import lax
import numpy as np

D_MODEL = 4096
BATCH = 1
SEQ = 8192
DEPTH = 2

GRID_W = 64
CTX_LEN = 256

HEAD_DIM = 128
NA_HEADS = D_MODEL // (2 * HEAD_DIM)
DN_HEADS = D_MODEL // (2 * HEAD_DIM)
NA_DIM = NA_HEADS * HEAD_DIM
DN_DIM = DN_HEADS * HEAD_DIM
NA_WIN_ROWS = 8
NA_WIN_COLS = 16
DN_CONV = 5
DN_CHUNK = 64
ROPE_BASE = 10000.0
IN0_COLS = 3 * NA_DIM + 4 * DN_DIM + 4 * DN_HEADS

S5_GROUP = 16
S5_GROUPS = D_MODEL // S5_GROUP
S5_STATE = 64
S5_BLOCK = min(32, S5_GROUPS)

D_FF = 2 * D_MODEL
MACARON_WEIGHT = 0.5
N_MOD = 9
LN_EPS = 1e-5
N_EVEN = (DEPTH + 1) // 2
N_ODD = DEPTH // 2
DEEPNORM_ALPHA = (2.0 * DEPTH) ** 0.25
DEEPNORM_BETA = (8.0 * DEPTH) ** -0.25

kernel_name = "hybrid_na_gdn_s5_macaron_deepnorm_prefix"


def _layernorm(z, g, b):
    zf = z.astype(jnp.float32)
    mu = jnp.mean(zf, -1, keepdims=True)
    var = jnp.mean(jnp.square(zf - mu), -1, keepdims=True)
    return ((zf - mu) * lax.rsqrt(var + LN_EPS) * g.astype(jnp.float32) + b.astype(jnp.float32)).astype(z.dtype)


def _rmsnorm(z, g, eps=1e-6):
    zf = z.astype(jnp.float32)
    return zf * lax.rsqrt(jnp.mean(zf * zf, -1, keepdims=True) + eps) * g.astype(jnp.float32)


def _l2norm(z, eps=1e-6):
    zf = z.astype(jnp.float32)
    return zf * lax.rsqrt(jnp.sum(zf * zf, -1, keepdims=True) + eps)


def _modulate(z, mod, k):
    return z * (1.0 + mod[:, 3 * k + 1]) + mod[:, 3 * k]


def _residual(z, y, mod, k, g, b, weight):
    return _layernorm(DEEPNORM_ALPHA * z + weight * mod[:, 3 * k + 2] * y, g, b)


def _ffn_sublayer(z, mod, k, w_in, w_out, g, b):
    gu = _modulate(z, mod, k) @ w_in
    gate, up = jnp.split(gu, 2, axis=-1)
    y = (jax.nn.silu(gate) * up) @ w_out
    return _residual(z, y, mod, k, g, b, MACARON_WEIGHT)


def _seq_dir(a, d):
    return a if d == 0 else jnp.flip(a, axis=1)


def _axial_rope(z):
    B, T, H, E = z.shape
    t = jnp.arange(T)
    n_freq = E // 4
    inv = ROPE_BASE ** (-jnp.arange(n_freq, dtype=jnp.float32) / n_freq)
    ang = jnp.concatenate([(t // GRID_W).astype(jnp.float32)[:, None] * inv,
                           (t % GRID_W).astype(jnp.float32)[:, None] * inv], -1)
    cos, sin = jnp.cos(ang)[None, :, None, :], jnp.sin(ang)[None, :, None, :]
    z2 = z.reshape(B, T, H, E // 2, 2)
    z1, zz = z2[..., 0], z2[..., 1]
    return jnp.stack([z1 * cos - zz * sin, z1 * sin + zz * cos], -1).reshape(B, T, H, E)


def _short_conv(z, w):
    pad = w.shape[0] // 2
    return lax.conv_general_dilated(z, w[:, None, :].astype(z.dtype), window_strides=(1,), padding=[(pad, pad)],
                                    dimension_numbers=('NWC', 'WIO', 'NWC'), feature_group_count=z.shape[-1])


def _softmax_attention(q, k, v):
    s = jnp.einsum('bqhe,bkhe->bhqk', q, k).astype(jnp.float32) * (q.shape[-1] ** -0.5)
    p = jax.nn.softmax(s, -1).astype(v.dtype)
    return jnp.einsum('bhqk,bkhe->bqhe', p, v)


def _neighbourhood_attention(q, k, v, k_ctx, v_ctx, rpb):
    B, T, H, E = q.shape
    rows = T // GRID_W
    kr, kw = min(NA_WIN_ROWS, rows), NA_WIN_COLS
    scale = E ** -0.5
    qg, kg, vg = (a.reshape(B, rows, GRID_W, H, E) for a in (q, k, v))
    col = jnp.arange(GRID_W)
    col_idx = jnp.clip(col - kw // 2, 0, GRID_W - kw)[:, None] + jnp.arange(kw)[None, :]
    col_bias_idx = col_idx - col[:, None] + (NA_WIN_COLS - 1)
    rpb = rpb.astype(jnp.float32)

    def row_block(r):
        r0 = jnp.clip(r - kr // 2, 0, rows - kr)
        q_r = lax.dynamic_index_in_dim(qg, r, axis=1, keepdims=False)
        k_r = lax.dynamic_slice_in_dim(kg, r0, kr, axis=1)[:, :, col_idx]
        v_r = lax.dynamic_slice_in_dim(vg, r0, kr, axis=1)[:, :, col_idx]
        row_bias_idx = r0 + jnp.arange(kr) - r + (NA_WIN_ROWS - 1)
        bias = rpb[:, row_bias_idx[None, :, None], col_bias_idx[:, None, :]]
        s_loc = jnp.einsum('bqhe,biqjhe->bhqij', q_r, k_r).astype(jnp.float32) * scale + bias
        s_ctx = jnp.einsum('bqhe,bkhe->bhqk', q_r, k_ctx).astype(jnp.float32) * scale
        s = jnp.concatenate([s_loc.reshape(B, H, GRID_W, kr * kw), s_ctx], -1)
        p = jax.nn.softmax(s, -1).astype(v.dtype)
        p_loc = p[..., :kr * kw].reshape(B, H, GRID_W, kr, kw)
        return (jnp.einsum('bhqij,biqjhe->bqhe', p_loc, v_r)
                + jnp.einsum('bhqk,bkhe->bqhe', p[..., kr * kw:], v_ctx))

    out = lax.map(row_block, jnp.arange(rows))
    return jnp.moveaxis(out, 0, 1).reshape(B, T, H, E)


def _gated_delta_rule(q, k, v, g, beta, s0):
    B, L, H, Dk = q.shape
    Dv = v.shape[-1]
    C = DN_CHUNK
    N = L // C

    def chunk(a):
        return jnp.moveaxis(a.reshape(B, N, C, H, *a.shape[3:]), (1, 3), (0, 2))

    q, k, v, g, beta = chunk(q * (Dk ** -0.5)), chunk(k), chunk(v), chunk(g), chunk(beta)
    gc = jnp.cumsum(g, axis=-1)
    idx = jnp.arange(C)
    incl = idx[:, None] >= idx[None, :]
    strict = idx[:, None] > idx[None, :]
    decay = jnp.exp(jnp.where(incl, gc[..., :, None] - gc[..., None, :], -jnp.inf))
    kb = k * beta[..., None]
    a_mat = jnp.where(strict, jnp.einsum('nbhid,nbhjd->nbhij', kb, k) * decay, 0.0)
    eye = jnp.eye(C, dtype=jnp.float32)
    t_inv = lax.linalg.triangular_solve(eye + a_mat, jnp.broadcast_to(eye, a_mat.shape),
                                        left_side=True, lower=True, unit_diagonal=True)
    u_c = t_inv @ (v * beta[..., None])
    w_c = t_inv @ (kb * jnp.exp(gc)[..., None])
    qk = jnp.einsum('nbhid,nbhjd->nbhij', q, k) * decay
    q_dec = q * jnp.exp(gc)[..., None]
    k_dec = k * jnp.exp(gc[..., -1:] - gc)[..., None]
    g_last = jnp.exp(gc[..., -1])

    def step(S, xs):
        qd, kd, uu, ww, qkm, gl = xs
        v_new = uu - jnp.einsum('bhcd,bhde->bhce', ww, S)
        o = jnp.einsum('bhcd,bhde->bhce', qd, S) + jnp.einsum('bhij,bhje->bhie', qkm, v_new)
        S = S * gl[..., None, None] + jnp.einsum('bhcd,bhce->bhde', kd, v_new)
        return S, o

    s_final, o = lax.scan(step, s0, (q_dec, k_dec, u_c, w_c, qk, g_last))
    return jnp.moveaxis(o, (0, 2), (1, 3)).reshape(B, L, H, Dv), s_final


def _mixer_na_gdn(u, uc, w_in, w_out, rpb, conv_w, a_log, dt_bias, norm_g, need_ctx):
    B, T, _ = u.shape

    def split(p):
        L = p.shape[1]
        o = 3 * NA_DIM
        na = p[..., :o].reshape(B, L, 3, NA_HEADS, HEAD_DIM)
        dn_qkv = p[..., o:o + 3 * DN_DIM]
        o += 3 * DN_DIM
        dn_gate = p[..., o:o + DN_DIM].reshape(B, L, DN_HEADS, HEAD_DIM)
        dn_ab = p[..., o + DN_DIM:].reshape(B, L, 4, DN_HEADS).astype(jnp.float32)
        return na, dn_qkv, dn_gate, dn_ab

    na_l, qkv_l, gate_l, ab_l = split(u @ w_in)
    na_c, qkv_c, gate_c, ab_c = split(uc @ w_in)

    o_na = _neighbourhood_attention(na_l[:, :, 0], na_l[:, :, 1], na_l[:, :, 2], na_c[:, :, 1], na_c[:, :, 2], rpb)

    def dn_inputs(qkv, ab, on_grid):
        L = qkv.shape[1]
        qkv = jax.nn.silu(_short_conv(qkv, conv_w)).reshape(B, L, 3, DN_HEADS, HEAD_DIM)
        q, k = _l2norm(qkv[:, :, 0]), _l2norm(qkv[:, :, 1])
        if on_grid:
            q, k = _axial_rope(q), _axial_rope(k)
        v = qkv[:, :, 2].astype(jnp.float32)
        gs = [-jnp.exp(a_log[d].astype(jnp.float32)) * jax.nn.softplus(ab[:, :, d] + dt_bias[d].astype(jnp.float32))
              for d in range(2)]
        betas = [jax.nn.sigmoid(ab[:, :, 2 + d]) for d in range(2)]
        return q, k, v, gs, betas

    ql, kl, vl, gl, bl = dn_inputs(qkv_l, ab_l, True)
    qc, kc, vc, gcx, bcx = dn_inputs(qkv_c, ab_c, False)
    o_dn_l = 0.0
    o_dn_c = 0.0
    for d in range(2):
        s0 = jnp.zeros((B, DN_HEADS, HEAD_DIM, HEAD_DIM), jnp.float32)
        oc, s_ctx = _gated_delta_rule(_seq_dir(qc, d), _seq_dir(kc, d), _seq_dir(vc, d),
                                      _seq_dir(gcx[d], d), _seq_dir(bcx[d], d), s0)
        ol, _ = _gated_delta_rule(_seq_dir(ql, d), _seq_dir(kl, d), _seq_dir(vl, d),
                                  _seq_dir(gl[d], d), _seq_dir(bl[d], d), s_ctx)
        o_dn_l = o_dn_l + _seq_dir(ol, d)
        if need_ctx:
            o_dn_c = o_dn_c + _seq_dir(oc, d)

    def dn_out(o, gate):
        o = _rmsnorm(o, norm_g) * jax.nn.silu(gate.astype(jnp.float32))
        return o.reshape(o.shape[0], o.shape[1], DN_DIM).astype(u.dtype)

    y = jnp.concatenate([o_na.reshape(B, T, NA_DIM), dn_out(o_dn_l, gate_l)], -1) @ w_out
    if not need_ctx:
        return y, None
    o_na_c = _softmax_attention(na_c[:, :, 0], na_c[:, :, 1], na_c[:, :, 2])
    yc = jnp.concatenate([o_na_c.reshape(B, uc.shape[1], NA_DIM), dn_out(o_dn_c, gate_c)], -1) @ w_out
    return y, yc


def _complex_affine_compose(e1, e2):
    a1r, a1i, b1r, b1i = e1
    a2r, a2i, b2r, b2i = e2
    return (a2r * a1r - a2i * a1i, a2r * a1i + a2i * a1r,
            a2r * b1r - a2i * b1i + b2r, a2r * b1i + a2i * b1r + b2i)


def _diag_ssm_scan(u, lb_re, lb_im, bb_re, bb_im, x0):
    bu_re = jnp.einsum('blgs,gps->blgp', u, bb_re)
    bu_im = jnp.einsum('blgs,gps->blgp', u, bb_im)
    if x0 is not None:
        x0r, x0i = x0
        bu_re = bu_re.at[:, 0].add(lb_re * x0r - lb_im * x0i)
        bu_im = bu_im.at[:, 0].add(lb_re * x0i + lb_im * x0r)
    a_re = jnp.broadcast_to(lb_re, bu_re.shape)
    a_im = jnp.broadcast_to(lb_im, bu_im.shape)
    _, _, x_re, x_im = lax.associative_scan(_complex_affine_compose, (a_re, a_im, bu_re, bu_im), axis=1)
    return x_re, x_im


def _ssm_readout(x_re, x_im, c_re, c_im):
    return jnp.einsum('blgp,gsp->blgs', x_re, c_re) - jnp.einsum('blgp,gsp->blgs', x_im, c_im)


def _mixer_s5(u, uc, lam_re, lam_im, log_step, b_re, b_im, c_re, c_im, d_skip, w_out, w_gate, need_ctx):
    B, T, D = u.shape
    nb = S5_GROUPS // S5_BLOCK
    dt = jnp.exp(log_step.astype(jnp.float32))[..., None]
    lr, li = lam_re.astype(jnp.float32), lam_im.astype(jnp.float32)
    mag, ang = jnp.exp(lr * dt), li * dt
    lb_re, lb_im = mag * jnp.cos(ang), mag * jnp.sin(ang)
    den = lr * lr + li * li
    z_re = ((lb_re - 1.0) * lr + lb_im * li) / den
    z_im = (lb_im * lr - (lb_re - 1.0) * li) / den
    br, bi = b_re.astype(jnp.float32), b_im.astype(jnp.float32)
    bb_re = z_re[..., None] * br - z_im[..., None] * bi
    bb_im = z_re[..., None] * bi + z_im[..., None] * br

    def by_block(a):
        return jnp.moveaxis(a.reshape(2, nb, S5_BLOCK, *a.shape[2:]), 1, 0)

    def seq_block(z):
        return jnp.moveaxis(z.astype(jnp.float32).reshape(B, z.shape[1], nb, S5_BLOCK, S5_GROUP), 2, 0)

    params = tuple(by_block(a) for a in (lb_re, lb_im, bb_re, bb_im, c_re.astype(jnp.float32), c_im.astype(jnp.float32)))

    def block(args):
        ul, ucb, lbr, lbi, bbr, bbi, cr, ci = args
        y_l = 0.0
        y_c = 0.0
        for d in range(2):
            xcr, xci = _diag_ssm_scan(_seq_dir(ucb, d), lbr[d], lbi[d], bbr[d], bbi[d], None)
            xlr, xli = _diag_ssm_scan(_seq_dir(ul, d), lbr[d], lbi[d], bbr[d], bbi[d], (xcr[:, -1], xci[:, -1]))
            y_l = y_l + _seq_dir(_ssm_readout(xlr, xli, cr[d], ci[d]), d)
            if need_ctx:
                y_c = y_c + _seq_dir(_ssm_readout(xcr, xci, cr[d], ci[d]), d)
        return (y_l, y_c) if need_ctx else (y_l,)

    ys = lax.map(block, (seq_block(u), seq_block(uc)) + params)

    def head(z, yz):
        yz = jnp.moveaxis(yz, 0, 2).reshape(B, z.shape[1], D) + d_skip.astype(jnp.float32) * z.astype(jnp.float32)
        gz = jax.nn.gelu(yz).astype(z.dtype)
        return (gz @ w_out) * jax.nn.sigmoid(gz @ w_gate)

    y = head(u, ys[0])
    return (y, head(uc, ys[1])) if need_ctx else (y, None)


def setup_inputs(seed: int = 0) -> dict:
    key = jax.random.key(seed)
    ks = jax.random.split(key, 32)

    def nrm(k, shape, s):
        return s * jax.random.normal(k, shape, jnp.float32)

    G, P, S = S5_GROUPS, S5_STATE, S5_GROUP
    n_idx = jnp.arange(P, dtype=jnp.float32)
    dt0 = jnp.exp(jax.random.uniform(ks[15], (N_EVEN, 2, DN_HEADS), jnp.float32, math.log(1e-3), math.log(1e-1)))
    return {
        "x": nrm(ks[0], (BATCH, SEQ, D_MODEL), 1.0),
        "c": nrm(ks[1], (BATCH, D_MODEL), 1.0),
        "ctx": nrm(ks[2], (BATCH, CTX_LEN, D_MODEL), 1.0),
        "c_ctx": nrm(ks[3], (D_MODEL,), 1.0),
        "w_mod": nrm(ks[4], (DEPTH, D_MODEL, N_MOD * D_MODEL), 0.5 * D_MODEL ** -0.5),
        "b_mod": nrm(ks[5], (DEPTH, N_MOD * D_MODEL), 0.02),
        "ln_g": 1.0 + nrm(ks[6], (DEPTH, 3, D_MODEL), 0.02),
        "ln_b": nrm(ks[7], (DEPTH, 3, D_MODEL), 0.02),
        "ffn_w_in": nrm(ks[8], (DEPTH, 2, D_MODEL, 2 * D_FF), D_MODEL ** -0.5),
        "ffn_w_out": nrm(ks[9], (DEPTH, 2, D_FF, D_MODEL), DEEPNORM_BETA * D_FF ** -0.5),
        "ab_w_in": nrm(ks[10], (N_EVEN, D_MODEL, IN0_COLS), D_MODEL ** -0.5),
        "ab_w_out": nrm(ks[11], (N_EVEN, NA_DIM + DN_DIM, D_MODEL), DEEPNORM_BETA * (NA_DIM + DN_DIM) ** -0.5),
        "na_rpb": nrm(ks[12], (N_EVEN, NA_HEADS, 2 * NA_WIN_ROWS - 1, 2 * NA_WIN_COLS - 1), 0.1),
        "dn_conv_w": nrm(ks[13], (N_EVEN, DN_CONV, 3 * DN_DIM), DN_CONV ** -0.5),
        "dn_a_log": jnp.log(jax.random.uniform(ks[14], (N_EVEN, 2, DN_HEADS), jnp.float32, 1.0, 16.0)),
        "dn_dt_bias": dt0 + jnp.log(-jnp.expm1(-dt0)),
        "dn_norm_g": 1.0 + nrm(ks[16], (N_EVEN, HEAD_DIM), 0.02),
        "s5_lam_re": -0.5 + nrm(ks[17], (N_ODD, 2, G, P), 0.01),
        "s5_lam_im": math.pi * n_idx + nrm(ks[18], (N_ODD, 2, G, P), 0.01),
        "s5_log_step": jax.random.uniform(ks[19], (N_ODD, 2, G), jnp.float32, math.log(1e-3), math.log(1e-1)),
        "s5_b_re": nrm(ks[20], (N_ODD, 2, G, P, S), (2 * S) ** -0.5),
        "s5_b_im": nrm(ks[21], (N_ODD, 2, G, P, S), (2 * S) ** -0.5),
        "s5_c_re": nrm(ks[22], (N_ODD, 2, G, S, P), (2 * P) ** -0.5),
        "s5_c_im": nrm(ks[23], (N_ODD, 2, G, S, P), (2 * P) ** -0.5),
        "s5_d": nrm(ks[24], (N_ODD, D_MODEL), 1.0),
        "s5_w_out": nrm(ks[25], (N_ODD, D_MODEL, D_MODEL), DEEPNORM_BETA * D_MODEL ** -0.5),
        "s5_w_gate": nrm(ks[26], (N_ODD, D_MODEL, D_MODEL), D_MODEL ** -0.5),
    }


def reference(x, c, ctx, c_ctx, w_mod, b_mod, ln_g, ln_b, ffn_w_in, ffn_w_out, ab_w_in, ab_w_out, na_rpb,
              dn_conv_w, dn_a_log, dn_dt_bias, dn_norm_g, s5_lam_re, s5_lam_im, s5_log_step, s5_b_re, s5_b_im,
              s5_c_re, s5_c_im, s5_d, s5_w_out, s5_w_gate):
    B, _, D = x.shape
    h, hc = x, ctx
    for i in range(DEPTH):
        last = i == DEPTH - 1
        m = (jax.nn.silu(c) @ w_mod[i] + b_mod[i]).reshape(B, N_MOD, 1, D)
        mc = (jax.nn.silu(c_ctx) @ w_mod[i] + b_mod[i]).reshape(1, N_MOD, 1, D)
        h = _ffn_sublayer(h, m, 0, ffn_w_in[i, 0], ffn_w_out[i, 0], ln_g[i, 0], ln_b[i, 0])
        hc = _ffn_sublayer(hc, mc, 0, ffn_w_in[i, 0], ffn_w_out[i, 0], ln_g[i, 0], ln_b[i, 0])
        u, uc = _modulate(h, m, 1), _modulate(hc, mc, 1)
        if i % 2 == 0:
            e = i // 2
            y, yc = _mixer_na_gdn(u, uc, ab_w_in[e], ab_w_out[e], na_rpb[e], dn_conv_w[e], dn_a_log[e],
                                  dn_dt_bias[e], dn_norm_g[e], not last)
        else:
            o = i // 2
            y, yc = _mixer_s5(u, uc, s5_lam_re[o], s5_lam_im[o], s5_log_step[o], s5_b_re[o], s5_b_im[o],
                              s5_c_re[o], s5_c_im[o], s5_d[o], s5_w_out[o], s5_w_gate[o], not last)
        h = _residual(h, y, m, 1, ln_g[i, 1], ln_b[i, 1], 1.0)
        h = _ffn_sublayer(h, m, 2, ffn_w_in[i, 1], ffn_w_out[i, 1], ln_g[i, 2], ln_b[i, 2])
        if not last:
            hc = _residual(hc, yc, mc, 1, ln_g[i, 1], ln_b[i, 1], 1.0)
            hc = _ffn_sublayer(hc, mc, 2, ffn_w_in[i, 1], ffn_w_out[i, 1], ln_g[i, 2], ln_b[i, 2])
    return h
```

```python
import functools
import math

import numpy as np
import jax
import jax.numpy as jnp
from jax import lax
from jax.experimental import pallas as pl
from jax.experimental.pallas import tpu as pltpu

GRID_W = 64
HEAD_DIM = 128
N_HEADS = 16
NA_DIM = N_HEADS * HEAD_DIM
DN_DIM = N_HEADS * HEAD_DIM
NA_WIN_ROWS = 8
NA_WIN_COLS = 16
DN_CONV = 5
DN_CHUNK = 64
ROPE_BASE = 10000.0
S5_GROUP = 16
S5_STATE = 64
S5_CHUNK = 16
N_MOD = 9
LN_EPS = 1e-5
MACARON_WEIGHT = 0.5

V7X_VMEM_BYTES = 64 * 1024 * 1024
VMEM_LIMIT = V7X_VMEM_BYTES - 8 * 1024 * 1024
LANES = 128
SUBLANES = 8

F32 = jnp.float32
BF16 = jnp.bfloat16


def _cparams(sem):
    return pltpu.CompilerParams(dimension_semantics=sem, vmem_limit_bytes=VMEM_LIMIT)


def _silu(x):
    return x * (1.0 / (1.0 + jnp.exp(-x)))


def _sigmoid(x):
    return 1.0 / (1.0 + jnp.exp(-x))


def _row_tile(rows):
    for t in (768, 512, 256):
        if rows % t == 0:
            return t
    raise ValueError(f"unsupported row count {rows}")


def _mod_kernel(c_ref, w_ref, b_ref, o_ref):
    s = _silu(c_ref[...]).astype(BF16)
    w = w_ref[0].astype(BF16)
    o_ref[0] = jnp.dot(s, w, preferred_element_type=F32) + b_ref[0]


def _modulation(c, c_ctx, w_mod, b_mod):
    depth, d, n = w_mod.shape
    tn = 512
    cc = jnp.zeros((SUBLANES, d), F32).at[0].set(c[0]).at[1].set(c_ctx)
    out = pl.pallas_call(
        _mod_kernel,
        grid=(depth, n // tn),
        in_specs=[
            pl.BlockSpec((SUBLANES, d), lambda l, j: (0, 0)),
            pl.BlockSpec((1, d, tn), lambda l, j: (l, 0, j)),
            pl.BlockSpec((1, 1, tn), lambda l, j: (l, 0, j)),
        ],
        out_specs=pl.BlockSpec((1, SUBLANES, tn), lambda l, j: (l, 0, j)),
        out_shape=jax.ShapeDtypeStruct((depth, SUBLANES, n), F32),
        compiler_params=_cparams(("arbitrary", "arbitrary")),
        name="modulation",
    )(cc, w_mod, b_mod.reshape(depth, 1, n))
    return out[:, :2].reshape(depth * 2 * N_MOD, 1, d)


def _mm_kernel(a_ref, w_ref, o_ref):
    o_ref[...] = jnp.dot(a_ref[...], w_ref[...], preferred_element_type=F32).astype(o_ref.dtype)


def _matmul(a, w, *, col0, ncols, tn, out_dtype):
    r, k = a.shape
    tm = _row_tile(r)
    assert col0 % tn == 0 and ncols % tn == 0 and w.shape[0] == k
    jb = col0 // tn
    return pl.pallas_call(
        _mm_kernel,
        grid=(r // tm, ncols // tn),
        in_specs=[
            pl.BlockSpec((tm, k), lambda i, j: (i, 0)),
            pl.BlockSpec((k, tn), lambda i, j: (0, j + jb)),
        ],
        out_specs=pl.BlockSpec((tm, tn), lambda i, j: (i, j)),
        out_shape=jax.ShapeDtypeStruct((r, ncols), out_dtype),
        compiler_params=_cparams(("arbitrary", "arbitrary")),
        name="matmul",
    )(a, w)


def _mm_pair_kernel(a_ref, w1_ref, w2_ref, o_ref, *, gate_second):
    a = a_ref[...]
    p1 = jnp.dot(a, w1_ref[...], preferred_element_type=F32)
    p2 = jnp.dot(a, w2_ref[...], preferred_element_type=F32)
    if gate_second:
        o = p1 * _sigmoid(p2)
    else:
        o = _silu(p1) * p2
    o_ref[...] = o.astype(o_ref.dtype)


def _matmul_pair(a, w1, w2, *, col1, col2, ncols, tn, gate_second, out_dtype):
    r, k = a.shape
    tm = _row_tile(r)
    assert col1 % tn == 0 and col2 % tn == 0 and ncols % tn == 0
    j1, j2 = col1 // tn, col2 // tn
    return pl.pallas_call(
        functools.partial(_mm_pair_kernel, gate_second=gate_second),
        grid=(r // tm, ncols // tn),
        in_specs=[
            pl.BlockSpec((tm, k), lambda i, j: (i, 0)),
            pl.BlockSpec((k, tn), lambda i, j: (0, j + j1)),
            pl.BlockSpec((k, tn), lambda i, j: (0, j + j2)),
        ],
        out_specs=pl.BlockSpec((tm, tn), lambda i, j: (i, j)),
        out_shape=jax.ShapeDtypeStruct((r, ncols), out_dtype),
        compiler_params=_cparams(("arbitrary", "arbitrary")),
        name="matmul_pair",
    )(a, w1, w2)


def _mm_cat_kernel(a1_ref, a2_ref, w_ref, o_ref, acc_ref):
    k = pl.program_id(2)

    @pl.when(k == 0)
    def _():
        acc_ref[...] = jnp.dot(a1_ref[...], w_ref[...], preferred_element_type=F32)

    @pl.when(k == 1)
    def _():
        o_ref[...] = (acc_ref[...] + jnp.dot(a2_ref[...], w_ref[...], preferred_element_type=F32)).astype(o_ref.dtype)


def _matmul_cat(a1, a2, w, *, tn, out_dtype):
    r, k1 = a1.shape
    assert a2.shape == (r, k1) and w.shape[0] == 2 * k1
    n = w.shape[1]
    tm = _row_tile(r)
    return pl.pallas_call(
        _mm_cat_kernel,
        grid=(r // tm, n // tn, 2),
        in_specs=[
            pl.BlockSpec((tm, k1), lambda i, j, k: (i, 0)),
            pl.BlockSpec((tm, k1), lambda i, j, k: (i, 0)),
            pl.BlockSpec((k1, tn), lambda i, j, k: (k, j)),
        ],
        out_specs=pl.BlockSpec((tm, tn), lambda i, j, k: (i, j)),
        out_shape=jax.ShapeDtypeStruct((r, n), out_dtype),
        scratch_shapes=[pltpu.VMEM((tm, tn), F32)],
        compiler_params=_cparams(("arbitrary", "arbitrary", "arbitrary")),
        name="matmul_cat",
    )(a1, a2, w)


ELT_ROWS = 256


def _modulate_kernel(z_ref, shift_ref, scale_ref, u_ref):
    u_ref[...] = (z_ref[...] * (1.0 + scale_ref[0]) + shift_ref[0]).astype(u_ref.dtype)


def _mod_row(layer, r, n_lat_tiles):
    return lambda i: ((layer * 2 + jnp.where(i >= n_lat_tiles, 1, 0)) * N_MOD + r, 0, 0)


def _modulate(z, mods, *, layer, sub, n_lat):
    r, d = z.shape
    nlt = n_lat // ELT_ROWS
    vec = lambda row: pl.BlockSpec((1, 1, d), _mod_row(layer, row, nlt))
    return pl.pallas_call(
        _modulate_kernel,
        grid=(r // ELT_ROWS,),
        in_specs=[pl.BlockSpec((ELT_ROWS, d), lambda i: (i, 0)), vec(3 * sub), vec(3 * sub + 1)],
        out_specs=pl.BlockSpec((ELT_ROWS, d), lambda i: (i, 0)),
        out_shape=jax.ShapeDtypeStruct((r, d), BF16),
        compiler_params=_cparams(("arbitrary",)),
        name="modulate",
    )(z, mods, mods)


def _res_ln_kernel(*refs, alpha, weight, n_u):
    z_ref, y_ref, gate_ref, g_ref, b_ref = refs[:5]
    t = alpha * z_ref[...] + (weight * gate_ref[0]) * y_ref[...].astype(F32)
    mu = jnp.mean(t, axis=-1, keepdims=True)
    tc = t - mu
    var = jnp.mean(tc * tc, axis=-1, keepdims=True)
    x = tc * lax.rsqrt(var + LN_EPS) * g_ref[0] + b_ref[0]
    if n_u == 0:
        refs[5][...] = x
        return
    shift_ref, scale_ref, x_ref = refs[5], refs[6], refs[7]
    x_ref[...] = x
    u = x * (1.0 + scale_ref[0]) + shift_ref[0]
    for u_ref in refs[8:]:
        u_ref[...] = u.astype(u_ref.dtype)


def _res_ln(z, y, *, mods, ln_g, ln_b, layer, sub, weight, alpha, n_lat, next_mod, u_dtypes=(BF16,)):
    r, d = y.shape
    nlt = n_lat // ELT_ROWS
    vec = lambda lyr, row: pl.BlockSpec((1, 1, d), _mod_row(lyr, row, nlt))
    ln_idx = layer * 3 + sub
    ln_spec = pl.BlockSpec((1, 1, d), lambda i: (ln_idx, 0, 0))
    row_spec = pl.BlockSpec((ELT_ROWS, d), lambda i: (i, 0))
    in_specs = [row_spec, row_spec, vec(layer, 3 * sub + 2), ln_spec, ln_spec]
    args = [z, y, mods, ln_g, ln_b]
    out_specs = [row_spec]
    out_shape = [jax.ShapeDtypeStruct((r, d), F32)]
    n_u = 0
    if next_mod is not None:
        nl, ns = next_mod
        in_specs += [vec(nl, 3 * ns), vec(nl, 3 * ns + 1)]
        args += [mods, mods]
        n_u = len(u_dtypes)
        out_specs += [row_spec] * n_u
        out_shape += [jax.ShapeDtypeStruct((r, d), dt) for dt in u_dtypes]
    res = pl.pallas_call(
        functools.partial(_res_ln_kernel, alpha=alpha, weight=weight, n_u=n_u),
        grid=(r // ELT_ROWS,),
        in_specs=in_specs,
        out_specs=out_specs,
        out_shape=out_shape,
        compiler_params=_cparams(("arbitrary",)),
        name="res_ln",
    )(*args)
    return res


def _ffn(u, w_in, w_out):
    d_ff = w_out.shape[0]
    hmid = _matmul_pair(u, w_in, w_in, col1=0, col2=d_ff, ncols=d_ff, tn=512, gate_second=False, out_dtype=BF16)
    return _matmul(hmid, w_out, col0=0, ncols=w_out.shape[1], tn=512, out_dtype=BF16)


NEG_BIG = -1e30
_NT = (((1,), (1,)), ((), ()))


def _na_bias_table(rpb):
    kr, kw, w = NA_WIN_ROWS, NA_WIN_COLS, GRID_W
    pat = np.arange(kr)[:, None]
    i = np.arange(kr)[None, :]
    ridx = i - pat + (NA_WIN_ROWS - 1)
    col = np.arange(w)
    cs = np.clip(col - kw // 2, 0, w - kw)
    c2 = np.arange(w)[None, :]
    valid = (c2 >= cs[:, None]) & (c2 < cs[:, None] + kw)
    cidx = np.clip(c2 - col[:, None] + (NA_WIN_COLS - 1), 0, 2 * NA_WIN_COLS - 2)
    tbl = rpb.astype(F32)[:, ridx[:, None, :, None], cidx[None, :, None, :]]
    tbl = jnp.where(jnp.asarray(valid)[None, None, :, None, :], tbl, NEG_BIG)
    return tbl.reshape(rpb.shape[0], kr, w, kr * w)


def _na_kernel(q_ref, k_ref, v_ref, bias_ref, o_ref, *, n_lat, n_ctx):
    rows = n_lat // GRID_W
    kr = NA_WIN_ROWS
    scale = HEAD_DIM ** -0.5
    kc = k_ref[pl.ds(n_lat, n_ctx), :]
    vc = v_ref[pl.ds(n_lat, n_ctx), :]

    def attend(q, s_parts, v_parts):
        m = functools.reduce(jnp.maximum, [jnp.max(s, axis=-1, keepdims=True) for s in s_parts])
        ps = [jnp.exp(s - m) for s in s_parts]
        den = functools.reduce(lambda a, b: a + b, [jnp.sum(p, axis=-1, keepdims=True) for p in ps])
        num = functools.reduce(lambda a, b: a + b,
                               [jnp.dot(p.astype(BF16), v, preferred_element_type=F32) for p, v in zip(ps, v_parts)])
        return num / den

    def body(r, carry):
        r0 = jnp.clip(r - kr // 2, 0, rows - kr)
        q = q_ref[pl.ds(pl.multiple_of(r * GRID_W, GRID_W), GRID_W), :]
        kstart = pl.multiple_of(r0 * GRID_W, GRID_W)
        kw = k_ref[pl.ds(kstart, kr * GRID_W), :]
        vw = v_ref[pl.ds(kstart, kr * GRID_W), :]
        s_loc = lax.dot_general(q, kw, _NT, preferred_element_type=F32) * scale + bias_ref[0, r - r0]
        s_ctx = lax.dot_general(q, kc, _NT, preferred_element_type=F32) * scale
        o = attend(q, [s_loc, s_ctx], [vw, vc])
        o_ref[pl.ds(pl.multiple_of(r * GRID_W, GRID_W), GRID_W), :] = o.astype(o_ref.dtype)
        return carry

    lax.fori_loop(0, rows, body, 0)
    qc = q_ref[pl.ds(n_lat, n_ctx), :]
    s_cc = lax.dot_general(qc, kc, _NT, preferred_element_type=F32) * scale
    o_ref[pl.ds(n_lat, n_ctx), :] = attend(qc, [s_cc], [vc]).astype(o_ref.dtype)


def _na_attention(qkv, bias_tbl, *, n_lat):
    r = qkv.shape[0]
    n_ctx = r - n_lat
    assert n_lat // GRID_W >= NA_WIN_ROWS
    blk = lambda off: pl.BlockSpec((r, HEAD_DIM), lambda h: (0, h + off))
    return pl.pallas_call(
        functools.partial(_na_kernel, n_lat=n_lat, n_ctx=n_ctx),
        grid=(N_HEADS,),
        in_specs=[blk(0), blk(N_HEADS), blk(2 * N_HEADS),
                  pl.BlockSpec((1,) + bias_tbl.shape[1:], lambda h: (h, 0, 0, 0))],
        out_specs=pl.BlockSpec((r, HEAD_DIM), lambda h: (0, h)),
        out_shape=jax.ShapeDtypeStruct((r, NA_DIM), BF16),
        compiler_params=_cparams(("arbitrary",)),
        name="na_attention",
    )(qkv, qkv, qkv, bias_tbl)


CONV_ROWS = 256
PAD = SUBLANES


def _rope_tables(n_lat):
    t = np.arange(n_lat)
    n_freq = HEAD_DIM // 4
    inv = ROPE_BASE ** (-np.arange(n_freq, dtype=np.float32) / n_freq)
    ang = np.concatenate([(t // GRID_W).astype(np.float32)[:, None] * inv,
                          (t % GRID_W).astype(np.float32)[:, None] * inv], -1).astype(np.float32)
    ang = jnp.asarray(ang)
    cos = jnp.repeat(jnp.cos(ang), 2, axis=-1)
    sin = jnp.repeat(jnp.sin(ang), 2, axis=-1) * jnp.asarray(np.tile(np.array([-1.0, 1.0], np.float32), HEAD_DIM // 2))
    return cos, sin


def _dn_conv_kernel(x_ref, w_ref, cos_ref, sin_ref, o_ref, xp_ref, *, n_lat, n_ctx, qk):
    half = DN_CONV // 2
    zeros = jnp.zeros((PAD, LANES), F32)
    xp_ref[pl.ds(0, PAD), :] = zeros
    xp_ref[pl.ds(PAD, n_lat), :] = x_ref[pl.ds(0, n_lat), :]
    xp_ref[pl.ds(PAD + n_lat, PAD), :] = zeros
    xp_ref[pl.ds(2 * PAD + n_lat, n_ctx), :] = x_ref[pl.ds(n_lat, n_ctx), :]
    xp_ref[pl.ds(2 * PAD + n_lat + n_ctx, PAD), :] = zeros
    w = w_ref[...]
    lane = lax.broadcasted_iota(jnp.int32, (CONV_ROWS, LANES), 1)
    even = (lane % 2) == 0

    def chunk(src0, dst0, rope_row0):
        big = xp_ref[pl.ds(src0 - PAD, CONV_ROWS + 2 * PAD), :]
        acc = jnp.zeros((CONV_ROWS, LANES), F32)
        for j in range(DN_CONV):
            acc = acc + big[PAD + j - half: PAD + j - half + CONV_ROWS, :] * w[j:j + 1, :]
        y = _silu(acc)
        if qk:
            y = y * lax.rsqrt(jnp.sum(y * y, axis=-1, keepdims=True) + 1e-6)
            if rope_row0 is not None:
                swapped = jnp.where(even, pltpu.roll(y, LANES - 1, 1), pltpu.roll(y, 1, 1))
                y = y * cos_ref[pl.ds(rope_row0, CONV_ROWS), :] + swapped * sin_ref[pl.ds(rope_row0, CONV_ROWS), :]
        o_ref[pl.ds(dst0, CONV_ROWS), :] = y

    def lat_body(i, carry):
        r0 = pl.multiple_of(i * CONV_ROWS, CONV_ROWS)
        chunk(r0 + PAD, r0, r0)
        return carry

    lax.fori_loop(0, n_lat // CONV_ROWS, lat_body, 0)
    for i in range(n_ctx // CONV_ROWS):
        chunk(2 * PAD + n_lat + i * CONV_ROWS, n_lat + i * CONV_ROWS, None)


def _dn_conv(qkv, conv_w, cos, sin, *, n_lat, col0, ncols, qk):
    r = qkv.shape[0]
    n_ctx = r - n_lat
    assert n_lat % CONV_ROWS == 0 and n_ctx % CONV_ROWS == 0
    jb = col0 // LANES
    return pl.pallas_call(
        functools.partial(_dn_conv_kernel, n_lat=n_lat, n_ctx=n_ctx, qk=qk),
        grid=(ncols // LANES,),
        in_specs=[
            pl.BlockSpec((r, LANES), lambda j: (0, j + jb)),
            pl.BlockSpec((DN_CONV, LANES), lambda j: (0, j + jb)),
            pl.BlockSpec((n_lat, LANES), lambda j: (0, 0)),
            pl.BlockSpec((n_lat, LANES), lambda j: (0, 0)),
        ],
        out_specs=pl.BlockSpec((r, LANES), lambda j: (0, j)),
        out_shape=jax.ShapeDtypeStruct((r, ncols), F32),
        scratch_shapes=[pltpu.VMEM((r + 3 * PAD, LANES), F32)],
        compiler_params=_cparams(("arbitrary",)),
        name="dn_conv",
    )(qkv, conv_w, cos, sin)


def _dn_gates_kernel(ab_ref, alog_ref, dtb_ref, o_ref):
    ab = ab_ref[...]
    z = ab + dtb_ref[...]
    softplus = jnp.maximum(z, 0.0) + jnp.log(1.0 + jnp.exp(-jnp.abs(z)))
    g = -jnp.exp(alog_ref[...]) * softplus
    lane = lax.broadcasted_iota(jnp.int32, ab.shape, 1)
    o_ref[...] = jnp.where(lane < 2 * N_HEADS, g, _sigmoid(ab))


def _dn_gates(ab, a_log, dt_bias):
    r = ab.shape[0]
    pad = lambda v: jnp.zeros((1, LANES), F32).at[0, :2 * N_HEADS].set(v.astype(F32).reshape(-1))
    return pl.pallas_call(
        _dn_gates_kernel,
        grid=(r // ELT_ROWS,),
        in_specs=[pl.BlockSpec((ELT_ROWS, LANES), lambda i: (i, 0)),
                  pl.BlockSpec((1, LANES), lambda i: (0, 0)),
                  pl.BlockSpec((1, LANES), lambda i: (0, 0))],
        out_specs=pl.BlockSpec((ELT_ROWS, LANES), lambda i: (i, 0)),
        out_shape=jax.ShapeDtypeStruct((r, LANES), F32),
        compiler_params=_cparams(("arbitrary",)),
        name="dn_gates",
    )(ab, pad(a_log), pad(dt_bias))


BLK = 2 * DN_CHUNK


def _split3(x):
    x1 = x.astype(BF16)
    r1 = x - x1.astype(F32)
    x2 = r1.astype(BF16)
    x3 = (r1 - x2.astype(F32)).astype(BF16)
    return x1, x2, x3


def _dot(a, b):
    return jnp.dot(a, b, preferred_element_type=F32)


def _dot_exact_left(m01, x):
    x1, x2, x3 = _split3(x)
    return _dot(m01, x1) + _dot(m01, x2) + _dot(m01, x3)


def _dot_exact_right(x, m01):
    x1, x2, x3 = _split3(x)
    return _dot(x1, m01) + _dot(x2, m01) + _dot(x3, m01)


def _dot_hi(a, b):
    a1 = a.astype(BF16)
    a2 = (a - a1.astype(F32)).astype(BF16)
    b1 = b.astype(BF16)
    b2 = (b - b1.astype(F32)).astype(BF16)
    return _dot(a1, b1) + _dot(a1, b2) + _dot(a2, b1)


def _dn_chunk_math(q, k, v, g_col, g_row, beta_col, *, rev):
    n = BLK
    i = lax.broadcasted_iota(jnp.int32, (n, n), 0)
    j = lax.broadcasted_iota(jnp.int32, (n, n), 1)
    same = (i // DN_CHUNK) == (j // DN_CHUNK)
    lower = same & (i >= j)
    upper = same & (i <= j)
    incl, strict = (upper, same & (i < j)) if rev else (lower, same & (i > j))
    one = lambda m: jnp.where(m, 1.0, 0.0).astype(BF16)
    m_col, m_row = (one(upper), one(lower)) if rev else (one(lower), one(upper))
    g_colb = jnp.broadcast_to(g_col, (n, n))
    gc_col = _dot_exact_left(m_col, g_colb)
    gc_row = _dot_exact_right(jnp.broadcast_to(g_row, (n, n)), m_row)
    tot_col = _dot_exact_left(one(same), g_colb)
    decay = jnp.where(incl, jnp.exp(jnp.where(incl, gc_col - gc_row, 0.0)), 0.0)
    kb = k * beta_col
    kbf = k.astype(BF16)
    a = jnp.where(strict, lax.dot_general(kb.astype(BF16), kbf, _NT, preferred_element_type=F32) * decay, 0.0)
    eye = jnp.where(i == j, 1.0, 0.0)
    t = eye - a
    p = a
    for _ in range(int(math.log2(DN_CHUNK)) - 1):
        p = _dot_hi(p, p)
        t = t + _dot_hi(t, p)
    eg = jnp.exp(gc_col)
    t1 = t.astype(BF16)
    t2 = (t - t1.astype(F32)).astype(BF16)
    vb = (v * beta_col).astype(BF16)
    kbg = (kb * eg).astype(BF16)
    u = _dot(t1, vb) + _dot(t2, vb)
    w = _dot(t1, kbg) + _dot(t2, kbg)
    qs = q * (HEAD_DIM ** -0.5)
    qk = jnp.where(incl, lax.dot_general(qs.astype(BF16), kbf, _NT, preferred_element_type=F32) * decay, 0.0)
    q_dec = qs * eg
    k_dec = k * jnp.exp(tot_col - gc_col)
    return u, w.astype(BF16), q_dec.astype(BF16), k_dec.T.astype(BF16), qk.astype(BF16)


PREP_BLOCKS = 2


def _dn_prep_kernel(q_ref, k_ref, v_ref, gcol_ref, grow_ref, bcol_ref,
                    u_ref, w_ref, qd_ref, kdt_ref, qk_ref):
    for d in range(2):
        for b in range(PREP_BLOCKS):
            rs = pl.ds(b * BLK, BLK)
            u, w, qd, kdt, qk = _dn_chunk_math(
                q_ref[rs, :], k_ref[rs, :], v_ref[rs, :],
                gcol_ref[d, 0, rs, :], grow_ref[d, 0, :, rs], bcol_ref[d, 0, rs, :], rev=(d == 1))
            u_ref[d, 0, rs, :] = u
            w_ref[d, 0, rs, :] = w
            qd_ref[d, 0, rs, :] = qd
            kdt_ref[d, 0, :, rs] = kdt
            qk_ref[d, 0, rs, :] = qk


def _dn_prep(qk, v, g_col, g_row, beta_col):
    r = v.shape[0]
    rb = PREP_BLOCKS * BLK
    assert r % rb == 0
    head_blk = pl.BlockSpec((rb, HEAD_DIM), lambda h, t: (t, h))
    k_blk = pl.BlockSpec((rb, HEAD_DIM), lambda h, t: (t, h + N_HEADS))
    col_blk = pl.BlockSpec((2, 1, rb, 1), lambda h, t: (0, h, t, 0))
    row_blk = pl.BlockSpec((2, 1, 1, rb), lambda h, t: (0, h, 0, t))
    out_blk = pl.BlockSpec((2, 1, rb, HEAD_DIM), lambda h, t: (0, h, t, 0))
    out_t_blk = pl.BlockSpec((2, 1, HEAD_DIM, rb), lambda h, t: (0, h, 0, t))
    sds = lambda shape, dt: jax.ShapeDtypeStruct((2, N_HEADS) + shape, dt)
    return pl.pallas_call(
        _dn_prep_kernel,
        grid=(N_HEADS, r // rb),
        in_specs=[head_blk, k_blk, head_blk, col_blk, row_blk, col_blk],
        out_specs=[out_blk, out_blk, out_blk, out_t_blk, out_blk],
        out_shape=[sds((r, HEAD_DIM), F32), sds((r, HEAD_DIM), BF16), sds((r, HEAD_DIM), BF16),
                   sds((HEAD_DIM, r), BF16), sds((r, HEAD_DIM), BF16)],
        compiler_params=_cparams(("arbitrary", "arbitrary")),
        name="dn_prep",
    )(qk, qk, v, g_col, g_row, beta_col)


SCAN_ROWS = 256


def _dn_scan_kernel(u0, u1, w0, w1, qd0, qd1, kdt0, kdt1, qk0, qk1, g0, g1, o0, o1, s_ref):
    @pl.when(pl.program_id(1) == 0)
    def _():
        s_ref[...] = jnp.zeros_like(s_ref)

    n_chunks = SCAN_ROWS // DN_CHUNK
    dirs = ((u0, w0, qd0, kdt0, qk0, g0, o0, range(n_chunks)),
            (u1, w1, qd1, kdt1, qk1, g1, o1, range(n_chunks - 1, -1, -1)))
    for d, (u_ref, w_ref, qd_ref, kdt_ref, qk_ref, g_ref, o_ref, order) in enumerate(dirs):
        s = s_ref[d]
        for c in order:
            rs = pl.ds(c * DN_CHUNK, DN_CHUNK)
            half = (c % 2) * DN_CHUNK
            wq = jnp.concatenate([w_ref[0, 0, rs, :], qd_ref[0, 0, rs, :]], axis=0)
            ws = _dot(wq, s.astype(BF16))
            vnb = (u_ref[0, 0, rs, :] - ws[:DN_CHUNK]).astype(BF16)
            qk = qk_ref[0, 0, rs, :][:, half:half + DN_CHUNK]
            o_ref[rs, :] = ws[DN_CHUNK:] + _dot(qk, vnb)
            g_last = jnp.exp(jnp.sum(g_ref[0, 0, rs, :], axis=0, keepdims=True))
            s = s * g_last + _dot(kdt_ref[0, 0, :, rs], vnb)
        s_ref[d] = s


def _dn_scan(u, w, qd, kdt, qk, g_col, *, n_lat):
    r = u.shape[2]
    nlb = n_lat // SCAN_ROWS
    assert r - n_lat == SCAN_ROWS
    rb = (lambda t: jnp.where(t == 0, nlb, t - 1), lambda t: jnp.where(t == 0, nlb, nlb - t))

    def pair(arr, shape, transposed=False):
        specs = []
        for d in range(2):
            if transposed:
                imap = lambda h, t, d=d: (d, h, 0, rb[d](t))
            else:
                imap = lambda h, t, d=d: (d, h, rb[d](t), 0)
            specs.append(pl.BlockSpec((1, 1) + shape, imap))
        return specs, [arr, arr]

    in_specs, args = [], []
    for arr, shape, tr in ((u, (SCAN_ROWS, HEAD_DIM), False), (w, (SCAN_ROWS, HEAD_DIM), False),
                           (qd, (SCAN_ROWS, HEAD_DIM), False), (kdt, (HEAD_DIM, SCAN_ROWS), True),
                           (qk, (SCAN_ROWS, HEAD_DIM), False), (g_col, (SCAN_ROWS, 1), False)):
        sp, ar = pair(arr, shape, tr)
        in_specs += sp
        args += ar
    out_specs = [pl.BlockSpec((SCAN_ROWS, HEAD_DIM), lambda h, t, d=d: (rb[d](t), h)) for d in range(2)]
    return pl.pallas_call(
        _dn_scan_kernel,
        grid=(N_HEADS, nlb + 1),
        in_specs=in_specs,
        out_specs=out_specs,
        out_shape=[jax.ShapeDtypeStruct((r, DN_DIM), F32)] * 2,
        scratch_shapes=[pltpu.VMEM((2, HEAD_DIM, HEAD_DIM), F32)],
        compiler_params=_cparams(("arbitrary", "arbitrary")),
        name="dn_scan",
    )(*args)


def _dn_out_kernel(of_ref, ob_ref, gate_ref, ng_ref, y_ref):
    for h in range(N_HEADS):
        cs = slice(h * HEAD_DIM, (h + 1) * HEAD_DIM)
        o = of_ref[:, cs] + ob_ref[:, cs]
        o = o * lax.rsqrt(jnp.mean(o * o, axis=-1, keepdims=True) + 1e-6) * ng_ref[...]
        y_ref[:, cs] = (o * _silu(gate_ref[:, cs])).astype(y_ref.dtype)


def _dn_out(o_f, o_b, gate, norm_g):
    r, n = o_f.shape
    row = pl.BlockSpec((ELT_ROWS, n), lambda i: (i, 0))
    return pl.pallas_call(
        _dn_out_kernel,
        grid=(r // ELT_ROWS,),
        in_specs=[row, row, row, pl.BlockSpec((1, HEAD_DIM), lambda i: (0, 0))],
        out_specs=row,
        out_shape=jax.ShapeDtypeStruct((r, n), BF16),
        compiler_params=_cparams(("arbitrary",)),
        name="dn_out",
    )(o_f, o_b, gate, norm_g.astype(F32).reshape(1, HEAD_DIM))


S5_GB = 8
S5_W = S5_CHUNK * S5_GROUP
S5_SL = 2 * S5_STATE
_HI = lax.Precision.HIGHEST


def _s5_operators(lam_re, lam_im, log_step, b_re, b_im, c_re, c_im):
    f = lambda a: a.astype(F32)
    lr, li, br, bi, cr, ci = map(f, (lam_re, lam_im, b_re, b_im, c_re, c_im))
    dt = jnp.exp(f(log_step))[..., None]
    mag, ang = jnp.exp(lr * dt), li * dt
    lbr, lbi = mag * jnp.cos(ang), mag * jnp.sin(ang)
    den = lr * lr + li * li
    zr = ((lbr - 1.0) * lr + lbi * li) / den
    zi = (lbi * lr - (lbr - 1.0) * li) / den
    bbr = zr[..., None] * br - zi[..., None] * bi
    bbi = zr[..., None] * bi + zi[..., None] * br
    c = S5_CHUNK
    pr, pi = [jnp.ones_like(lbr)], [jnp.zeros_like(lbi)]
    for _ in range(c):
        pr, pi = pr + [pr[-1] * lbr - pi[-1] * lbi], pi + [pr[-1] * lbi + pi[-1] * lbr]
    pr, pi = jnp.stack(pr), jnp.stack(pi)
    cpr = cr[None] * pr[:, :, :, None, :] - ci[None] * pi[:, :, :, None, :]
    cpi = cr[None] * pi[:, :, :, None, :] + ci[None] * pr[:, :, :, None, :]
    kl = (jnp.einsum('ldgsp,dgpt->ldgst', cpr, bbr, precision=_HI)
          - jnp.einsum('ldgsp,dgpt->ldgst', cpi, bbi, precision=_HI))
    i = np.arange(c)[:, None]
    j = np.arange(c)[None, :]
    g = lam_re.shape[1]
    mf = jnp.where(jnp.asarray(j >= i)[:, :, None, None, None], kl[np.clip(j - i, 0, c), 0], 0.0)
    mb = jnp.where(jnp.asarray(i >= j)[:, :, None, None, None], kl[np.clip(i - j, 0, c), 1], 0.0)
    m = jnp.transpose(mf + mb, (2, 0, 4, 1, 3)).reshape(g, S5_W, S5_W)
    kf = np.arange(c)[::-1].copy()
    kb = np.arange(c)

    def state_in(d, k):
        re = pr[k, d][:, :, :, None] * bbr[d][None] - pi[k, d][:, :, :, None] * bbi[d][None]
        im = pr[k, d][:, :, :, None] * bbi[d][None] + pi[k, d][:, :, :, None] * bbr[d][None]
        both = jnp.concatenate([re, im], axis=2)
        return jnp.transpose(both, (1, 0, 3, 2)).reshape(g, S5_W, S5_SL)

    def state_out(d, k):
        re, im = cpr[k, d], cpi[k, d]
        both = jnp.concatenate([re, -im], axis=-1)
        return jnp.transpose(both, (1, 3, 0, 2)).reshape(g, S5_SL, S5_W)

    bc = jnp.stack([state_in(0, kf), state_in(1, kb)])
    cc = jnp.stack([state_out(0, np.arange(1, c + 1)), state_out(1, c - np.arange(c))])
    ar, ai = pr[c], pi[c]

    def cpow(n):
        xr, xi = jnp.ones_like(ar), jnp.zeros_like(ai)
        for _ in range(n):
            xr, xi = xr * ar - xi * ai, xr * ai + xi * ar
        return xr, xi

    def rows(xr, xi):
        return jnp.stack([jnp.concatenate([xr, xr], -1), jnp.concatenate([-xi, xi], -1)], axis=2)

    steps = jnp.stack([jnp.broadcast_to(rows(*cpow(n))[:, :, :, None, :], (2, g, 2, SUBLANES, S5_SL)) for n in (1, 2, 4)], axis=2)
    pj = [rows(*cpow(n)) for n in range(1, SUBLANES + 1)]
    fwd = jnp.stack([pj[jj][0] for jj in range(SUBLANES)], axis=2)
    bwd = jnp.stack([pj[SUBLANES - 1 - jj][1] for jj in range(SUBLANES)], axis=2)
    carry = jnp.stack([fwd, bwd])[:, :, None]
    consts = jnp.concatenate([steps, carry], axis=2)
    return m.astype(BF16), bc.astype(BF16), cc.astype(BF16), consts


def _s5_kernel(ug_ref, m_ref, bc_ref, cc_ref, k_ref, y_ref, vf_ref, vb_ref, ef_ref, eb_ref, *, n_lat_c, n_ctx_c):
    n_all = n_lat_c + n_ctx_c
    for gi in range(S5_GB):
        ls = slice(gi * S5_SL, (gi + 1) * S5_SL)
        ug = ug_ref[gi]
        vf = _dot(ug, bc_ref[0, gi])
        vf_ref[pl.ds(0, n_ctx_c), ls] = vf[n_lat_c:]
        vf_ref[pl.ds(n_ctx_c, n_lat_c), ls] = vf[:n_lat_c]
        vb_ref[:, ls] = _dot(ug, bc_ref[1, gi])
    sub = lax.broadcasted_iota(jnp.int32, (SUBLANES, S5_SL), 0)
    n_tiles = n_all // SUBLANES

    def cmul(x, kre, kim):
        return x * kre + pltpu.roll(x, S5_STATE, 1) * kim

    def tile_scan(x, gi, d, carry_row):
        for n, sh in enumerate((1, 2, 4)):
            if d == 0:
                shifted = jnp.where(sub >= sh, pltpu.roll(x, sh, 0), 0.0)
            else:
                shifted = jnp.where(sub < SUBLANES - sh, pltpu.roll(x, SUBLANES - sh, 0), 0.0)
            x = x + cmul(shifted, k_ref[d, gi, n, 0], k_ref[d, gi, n, 1])
        cb = jnp.broadcast_to(carry_row, (SUBLANES, S5_SL))
        x = x + cmul(cb, k_ref[d, gi, 3, 0], k_ref[d, gi, 3, 1])
        if d == 0:
            excl = jnp.where(sub == 0, cb, pltpu.roll(x, 1, 0))
            return excl, x[SUBLANES - 1:SUBLANES, :]
        excl = jnp.where(sub == SUBLANES - 1, cb, pltpu.roll(x, SUBLANES - 1, 0))
        return excl, x[0:1, :]

    def body(t, carries):
        cf, cb = carries
        rf = pl.ds(pl.multiple_of(t * SUBLANES, SUBLANES), SUBLANES)
        rb = pl.ds(pl.multiple_of((n_tiles - 1 - t) * SUBLANES, SUBLANES), SUBLANES)
        ncf, ncb = [], []
        for gi in range(S5_GB):
            ls = slice(gi * S5_SL, (gi + 1) * S5_SL)
            e, c_new = tile_scan(vf_ref[rf, ls], gi, 0, cf[gi])
            ef_ref[rf, ls] = e
            ncf.append(c_new)
            e, c_new = tile_scan(vb_ref[rb, ls], gi, 1, cb[gi])
            eb_ref[rb, ls] = e
            ncb.append(c_new)
        return tuple(ncf), tuple(ncb)

    zero = tuple(jnp.zeros((1, S5_SL), F32) for _ in range(S5_GB))
    lax.fori_loop(0, n_tiles, body, (zero, zero))
    for gi in range(S5_GB):
        ls = slice(gi * S5_SL, (gi + 1) * S5_SL)
        ug = ug_ref[gi, pl.ds(0, n_lat_c), :]
        y = _dot(ug, m_ref[gi])
        y = y + _dot(ef_ref[pl.ds(n_ctx_c, n_lat_c), ls].astype(BF16), cc_ref[0, gi])
        y = y + _dot(eb_ref[pl.ds(0, n_lat_c), ls].astype(BF16), cc_ref[1, gi])
        y_ref[gi] = y


def _s5_scan(ug, m, bc, cc, consts, *, n_lat):
    g, n_all, _ = ug.shape
    n_lat_c = n_lat // S5_CHUNK
    n_ctx_c = n_all - n_lat_c
    assert g % S5_GB == 0 and n_lat_c % SUBLANES == 0 and n_ctx_c % SUBLANES == 0
    return pl.pallas_call(
        functools.partial(_s5_kernel, n_lat_c=n_lat_c, n_ctx_c=n_ctx_c),
        grid=(g // S5_GB,),
        in_specs=[
            pl.BlockSpec((S5_GB, n_all, S5_W), lambda i: (i, 0, 0)),
            pl.BlockSpec((S5_GB, S5_W, S5_W), lambda i: (i, 0, 0)),
            pl.BlockSpec((2, S5_GB, S5_W, S5_SL), lambda i: (0, i, 0, 0)),
            pl.BlockSpec((2, S5_GB, S5_SL, S5_W), lambda i: (0, i, 0, 0)),
            pl.BlockSpec((2, S5_GB, 4, 2, SUBLANES, S5_SL), lambda i: (0, i, 0, 0, 0, 0)),
        ],
        out_specs=pl.BlockSpec((S5_GB, n_lat_c, S5_W), lambda i: (i, 0, 0)),
        out_shape=jax.ShapeDtypeStruct((g, n_lat_c, S5_W), F32),
        scratch_shapes=[pltpu.VMEM((n_all, S5_GB * S5_SL), F32) for _ in range(4)],
        compiler_params=_cparams(("arbitrary",)),
        name="s5_scan",
    )(ug, m, bc, cc, consts)


def _gelu_skip_kernel(y_ref, u_ref, d_ref, o_ref):
    z = y_ref[...] + d_ref[...] * u_ref[...]
    inner = math.sqrt(2.0 / math.pi) * (z + 0.044715 * (z * z * z))
    o_ref[...] = (0.5 * z * (1.0 + jnp.tanh(inner))).astype(o_ref.dtype)


def _gelu_skip(y, u, d_skip):
    r, d = y.shape
    row = pl.BlockSpec((ELT_ROWS, d), lambda i: (i, 0))
    return pl.pallas_call(
        _gelu_skip_kernel,
        grid=(r // ELT_ROWS,),
        in_specs=[row, row, pl.BlockSpec((1, d), lambda i: (0, 0))],
        out_specs=row,
        out_shape=jax.ShapeDtypeStruct((r, d), BF16),
        compiler_params=_cparams(("arbitrary",)),
        name="gelu_skip",
    )(y, u, d_skip.astype(F32).reshape(1, d))


def _mixer_na_gdn(u, w_in, w_out, rpb, conv_w, a_log, dt_bias, norm_g, *, n_lat):
    r = u.shape[0]
    wb = w_in.astype(BF16)
    n_main = 3 * NA_DIM + 4 * DN_DIM
    qkv_na = _matmul(u, wb, col0=0, ncols=3 * NA_DIM, tn=512, out_dtype=BF16)
    qkv_dn = _matmul(u, wb, col0=3 * NA_DIM, ncols=3 * DN_DIM, tn=512, out_dtype=F32)
    gate = _matmul(u, wb, col0=3 * NA_DIM + 3 * DN_DIM, ncols=DN_DIM, tn=512, out_dtype=F32)
    w_ab = jnp.pad(w_in[:, n_main:], ((0, 0), (0, LANES - 4 * N_HEADS))).astype(BF16)
    ab = _matmul(u, w_ab, col0=0, ncols=LANES, tn=LANES, out_dtype=F32)

    o_na = _na_attention(qkv_na, _na_bias_table(rpb), n_lat=n_lat)

    cos, sin = _rope_tables(n_lat)
    qk = _dn_conv(qkv_dn, conv_w.astype(F32), cos, sin, n_lat=n_lat, col0=0, ncols=2 * DN_DIM, qk=True)
    v = _dn_conv(qkv_dn, conv_w.astype(F32), cos, sin, n_lat=n_lat, col0=2 * DN_DIM, ncols=DN_DIM, qk=False)
    gb = _dn_gates(ab, a_log, dt_bias)
    g_t = gb[:, :2 * N_HEADS].T
    b_t = gb[:, 2 * N_HEADS:4 * N_HEADS].T
    g_col = g_t.reshape(2, N_HEADS, r, 1)
    g_row = g_t.reshape(2, N_HEADS, 1, r)
    beta_col = b_t.reshape(2, N_HEADS, r, 1)
    u_c, w_c, qd, kdt, qkm = _dn_prep(qk, v, g_col, g_row, beta_col)
    o_f, o_b = _dn_scan(u_c, w_c, qd, kdt, qkm, g_col, n_lat=n_lat)
    o_dn = _dn_out(o_f, o_b, gate, norm_g)
    return _matmul_cat(o_na, o_dn, w_out.astype(BF16), tn=512, out_dtype=BF16)


def _mixer_s5(u_bf, u_f32, lam_re, lam_im, log_step, b_re, b_im, c_re, c_im, d_skip, w_out, w_gate, *, n_lat):
    r, d = u_bf.shape
    g = d // S5_GROUP
    ug = u_bf.reshape(r // S5_CHUNK, S5_CHUNK, g, S5_GROUP).transpose(2, 0, 1, 3).reshape(g, r // S5_CHUNK, S5_W)
    m, bc, cc, consts = _s5_operators(lam_re, lam_im, log_step, b_re, b_im, c_re, c_im)
    yg = _s5_scan(ug, m, bc, cc, consts, n_lat=n_lat)
    y = yg.reshape(g, n_lat // S5_CHUNK, S5_CHUNK, S5_GROUP).transpose(1, 2, 0, 3).reshape(n_lat, d)
    gz = _gelu_skip(y, u_f32, d_skip)
    return _matmul_pair(gz, w_out.astype(BF16), w_gate.astype(BF16), col1=0, col2=0, ncols=d, tn=512,
                        gate_second=True, out_dtype=BF16)


def kernel(x, c, ctx, c_ctx, w_mod, b_mod, ln_g, ln_b, ffn_w_in, ffn_w_out, ab_w_in, ab_w_out, na_rpb, dn_conv_w, dn_a_log, dn_dt_bias, dn_norm_g, s5_lam_re, s5_lam_im, s5_log_step, s5_b_re, s5_b_im, s5_c_re, s5_c_im, s5_d, s5_w_out, s5_w_gate):
    depth, d = w_mod.shape[0], x.shape[2]
    assert depth == 2 and x.shape[0] == 1, "layer 0 is the NA/DeltaNet layer, layer 1 the final S5 layer"
    n_lat = x.shape[1]
    alpha = (2.0 * depth) ** 0.25
    h = jnp.concatenate([x[0], ctx[0]], axis=0)
    mods = _modulation(c, c_ctx, w_mod, b_mod)
    lng = ln_g.astype(F32).reshape(depth * 3, 1, d)
    lnb = ln_b.astype(F32).reshape(depth * 3, 1, d)
    res_ln = functools.partial(_res_ln, mods=mods, ln_g=lng, ln_b=lnb, alpha=alpha, n_lat=n_lat)
    ffn = lambda uu, l, s: _ffn(uu, ffn_w_in[l, s].astype(BF16), ffn_w_out[l, s].astype(BF16))

    u = _modulate(h, mods, layer=0, sub=0, n_lat=n_lat)
    h, u = res_ln(h, ffn(u, 0, 0), layer=0, sub=0, weight=MACARON_WEIGHT, next_mod=(0, 1))
    y = _mixer_na_gdn(u, ab_w_in[0], ab_w_out[0], na_rpb[0], dn_conv_w[0], dn_a_log[0], dn_dt_bias[0], dn_norm_g[0],
                      n_lat=n_lat)
    h, u = res_ln(h, y, layer=0, sub=1, weight=1.0, next_mod=(0, 2))
    h, u = res_ln(h, ffn(u, 0, 1), layer=0, sub=2, weight=MACARON_WEIGHT, next_mod=(1, 0))

    h, u, u32 = res_ln(h, ffn(u, 1, 0), layer=1, sub=0, weight=MACARON_WEIGHT, next_mod=(1, 1), u_dtypes=(BF16, F32))
    y = _mixer_s5(u, u32, s5_lam_re[0], s5_lam_im[0], s5_log_step[0], s5_b_re[0], s5_b_im[0], s5_c_re[0], s5_c_im[0],
                  s5_d[0], s5_w_out[0], s5_w_gate[0], n_lat=n_lat)
    h, u = res_ln(h, y, layer=1, sub=1, weight=1.0, next_mod=(1, 2))
    (out,) = res_ln(h, ffn(u, 1, 1), layer=1, sub=2, weight=MACARON_WEIGHT, next_mod=None)
    return out[None]
```

```python
import functools
import math

import numpy as np
import jax
import jax.numpy as jnp
from jax import lax
from jax.experimental import pallas as pl
from jax.experimental.pallas import tpu as pltpu

GRID_W = 64
HEAD_DIM = 128
N_HEADS = 16
NA_DIM = N_HEADS * HEAD_DIM
DN_DIM = N_HEADS * HEAD_DIM
NA_WIN_ROWS = 8
NA_WIN_COLS = 16
DN_CONV = 5
DN_CHUNK = 64
ROPE_BASE = 10000.0
S5_GROUP = 16
S5_STATE = 64
S5_CHUNK = 16
N_MOD = 9
LN_EPS = 1e-5
MACARON_WEIGHT = 0.5

V7X_VMEM_BYTES = 64 * 1024 * 1024
VMEM_LIMIT = V7X_VMEM_BYTES - 8 * 1024 * 1024
LANES = 128
SUBLANES = 8

F32 = jnp.float32
BF16 = jnp.bfloat16


def _cparams(sem):
    return pltpu.CompilerParams(dimension_semantics=sem, vmem_limit_bytes=VMEM_LIMIT)


def _silu(x):
    return x * (1.0 / (1.0 + jnp.exp(-x)))


def _sigmoid(x):
    return 1.0 / (1.0 + jnp.exp(-x))


def _row_tile(rows):
    for t in (768, 512, 256):
        if rows % t == 0:
            return t
    raise ValueError(f"unsupported row count {rows}")


def _mod_kernel(c_ref, w_ref, b_ref, o_ref):
    s = _silu(c_ref[...]).astype(BF16)
    w = w_ref[0].astype(BF16)
    o_ref[0] = jnp.dot(s, w, preferred_element_type=F32) + b_ref[0]


def _modulation(c, c_ctx, w_mod, b_mod):
    depth, d, n = w_mod.shape
    tn = 512
    cc = jnp.zeros((SUBLANES, d), F32).at[0].set(c[0]).at[1].set(c_ctx)
    out = pl.pallas_call(
        _mod_kernel,
        grid=(depth, n // tn),
        in_specs=[
            pl.BlockSpec((SUBLANES, d), lambda l, j: (0, 0)),
            pl.BlockSpec((1, d, tn), lambda l, j: (l, 0, j)),
            pl.BlockSpec((1, 1, tn), lambda l, j: (l, 0, j)),
        ],
        out_specs=pl.BlockSpec((1, SUBLANES, tn), lambda l, j: (l, 0, j)),
        out_shape=jax.ShapeDtypeStruct((depth, SUBLANES, n), F32),
        compiler_params=_cparams(("arbitrary", "arbitrary")),
        name="modulation",
    )(cc, w_mod, b_mod.reshape(depth, 1, n))
    return out[:, :2].reshape(depth * 2 * N_MOD, 1, d)


def _mm_kernel(a_ref, w_ref, o_ref):
    o_ref[...] = jnp.dot(a_ref[...], w_ref[...], preferred_element_type=F32).astype(o_ref.dtype)


def _matmul(a, w, *, col0, ncols, tn, out_dtype):
    r, k = a.shape
    tm = _row_tile(r)
    assert col0 % tn == 0 and ncols % tn == 0 and w.shape[0] == k
    jb = col0 // tn
    return pl.pallas_call(
        _mm_kernel,
        grid=(r // tm, ncols // tn),
        in_specs=[
            pl.BlockSpec((tm, k), lambda i, j: (i, 0)),
            pl.BlockSpec((k, tn), lambda i, j: (0, j + jb)),
        ],
        out_specs=pl.BlockSpec((tm, tn), lambda i, j: (i, j)),
        out_shape=jax.ShapeDtypeStruct((r, ncols), out_dtype),
        compiler_params=_cparams(("arbitrary", "arbitrary")),
        name="matmul",
    )(a, w)


def _mm_pair_kernel(a_ref, w1_ref, w2_ref, o_ref, *, gate_second):
    a = a_ref[...]
    p1 = jnp.dot(a, w1_ref[...], preferred_element_type=F32)
    p2 = jnp.dot(a, w2_ref[...], preferred_element_type=F32)
    if gate_second:
        o = p1 * _sigmoid(p2)
    else:
        o = _silu(p1) * p2
    o_ref[...] = o.astype(o_ref.dtype)


def _matmul_pair(a, w1, w2, *, col1, col2, ncols, tn, gate_second, out_dtype):
    r, k = a.shape
    tm = _row_tile(r)
    assert col1 % tn == 0 and col2 % tn == 0 and ncols % tn == 0
    j1, j2 = col1 // tn, col2 // tn
    return pl.pallas_call(
        functools.partial(_mm_pair_kernel, gate_second=gate_second),
        grid=(r // tm, ncols // tn),
        in_specs=[
            pl.BlockSpec((tm, k), lambda i, j: (i, 0)),
            pl.BlockSpec((k, tn), lambda i, j: (0, j + j1)),
            pl.BlockSpec((k, tn), lambda i, j: (0, j + j2)),
        ],
        out_specs=pl.BlockSpec((tm, tn), lambda i, j: (i, j)),
        out_shape=jax.ShapeDtypeStruct((r, ncols), out_dtype),
        compiler_params=_cparams(("arbitrary", "arbitrary")),
        name="matmul_pair",
    )(a, w1, w2)


def _mm_cat_kernel(a1_ref, a2_ref, w_ref, o_ref, acc_ref):
    k = pl.program_id(2)

    @pl.when(k == 0)
    def _():
        acc_ref[...] = jnp.dot(a1_ref[...], w_ref[...], preferred_element_type=F32)

    @pl.when(k == 1)
    def _():
        o_ref[...] = (acc_ref[...] + jnp.dot(a2_ref[...], w_ref[...], preferred_element_type=F32)).astype(o_ref.dtype)


def _matmul_cat(a1, a2, w, *, tn, out_dtype):
    r, k1 = a1.shape
    assert a2.shape == (r, k1) and w.shape[0] == 2 * k1
    n = w.shape[1]
    tm = _row_tile(r)
    return pl.pallas_call(
        _mm_cat_kernel,
        grid=(r // tm, n // tn, 2),
        in_specs=[
            pl.BlockSpec((tm, k1), lambda i, j, k: (i, 0)),
            pl.BlockSpec((tm, k1), lambda i, j, k: (i, 0)),
            pl.BlockSpec((k1, tn), lambda i, j, k: (k, j)),
        ],
        out_specs=pl.BlockSpec((tm, tn), lambda i, j, k: (i, j)),
        out_shape=jax.ShapeDtypeStruct((r, n), out_dtype),
        scratch_shapes=[pltpu.VMEM((tm, tn), F32)],
        compiler_params=_cparams(("arbitrary", "arbitrary", "arbitrary")),
        name="matmul_cat",
    )(a1, a2, w)


ELT_ROWS = 256


def _modulate_kernel(z_ref, shift_ref, scale_ref, u_ref):
    u_ref[...] = (z_ref[...] * (1.0 + scale_ref[0]) + shift_ref[0]).astype(u_ref.dtype)


def _mod_row(layer, r, n_lat_tiles):
    return lambda i: ((layer * 2 + jnp.where(i >= n_lat_tiles, 1, 0)) * N_MOD + r, 0, 0)


def _modulate(z, mods, *, layer, sub, n_lat):
    r, d = z.shape
    nlt = n_lat // ELT_ROWS
    vec = lambda row: pl.BlockSpec((1, 1, d), _mod_row(layer, row, nlt))
    return pl.pallas_call(
        _modulate_kernel,
        grid=(r // ELT_ROWS,),
        in_specs=[pl.BlockSpec((ELT_ROWS, d), lambda i: (i, 0)), vec(3 * sub), vec(3 * sub + 1)],
        out_specs=pl.BlockSpec((ELT_ROWS, d), lambda i: (i, 0)),
        out_shape=jax.ShapeDtypeStruct((r, d), BF16),
        compiler_params=_cparams(("arbitrary",)),
        name="modulate",
    )(z, mods, mods)


def _res_ln_kernel(*refs, alpha, weight, n_u):
    z_ref, y_ref, gate_ref, g_ref, b_ref = refs[:5]
    t = alpha * z_ref[...] + (weight * gate_ref[0]) * y_ref[...].astype(F32)
    mu = jnp.mean(t, axis=-1, keepdims=True)
    tc = t - mu
    var = jnp.mean(tc * tc, axis=-1, keepdims=True)
    x = tc * lax.rsqrt(var + LN_EPS) * g_ref[0] + b_ref[0]
    if n_u == 0:
        refs[5][...] = x
        return
    shift_ref, scale_ref, x_ref = refs[5], refs[6], refs[7]
    x_ref[...] = x
    u = x * (1.0 + scale_ref[0]) + shift_ref[0]
    for u_ref in refs[8:]:
        u_ref[...] = u.astype(u_ref.dtype)


def _res_ln(z, y, *, mods, ln_g, ln_b, layer, sub, weight, alpha, n_lat, next_mod, u_dtypes=(BF16,)):
    r, d = y.shape
    nlt = n_lat // ELT_ROWS
    vec = lambda lyr, row: pl.BlockSpec((1, 1, d), _mod_row(lyr, row, nlt))
    ln_idx = layer * 3 + sub
    ln_spec = pl.BlockSpec((1, 1, d), lambda i: (ln_idx, 0, 0))
    row_spec = pl.BlockSpec((ELT_ROWS, d), lambda i: (i, 0))
    in_specs = [row_spec, row_spec, vec(layer, 3 * sub + 2), ln_spec, ln_spec]
    args = [z, y, mods, ln_g, ln_b]
    out_specs = [row_spec]
    out_shape = [jax.ShapeDtypeStruct((r, d), F32)]
    n_u = 0
    if next_mod is not None:
        nl, ns = next_mod
        in_specs += [vec(nl, 3 * ns), vec(nl, 3 * ns + 1)]
        args += [mods, mods]
        n_u = len(u_dtypes)
        out_specs += [row_spec] * n_u
        out_shape += [jax.ShapeDtypeStruct((r, d), dt) for dt in u_dtypes]
    res = pl.pallas_call(
        functools.partial(_res_ln_kernel, alpha=alpha, weight=weight, n_u=n_u),
        grid=(r // ELT_ROWS,),
        in_specs=in_specs,
        out_specs=out_specs,
        out_shape=out_shape,
        compiler_params=_cparams(("arbitrary",)),
        name="res_ln",
    )(*args)
    return res


def _ffn(u, w_in, w_out):
    d_ff = w_out.shape[0]
    hmid = _matmul_pair(u, w_in, w_in, col1=0, col2=d_ff, ncols=d_ff, tn=512, gate_second=False, out_dtype=BF16)
    return _matmul(hmid, w_out, col0=0, ncols=w_out.shape[1], tn=512, out_dtype=BF16)


NEG_BIG = -1e30
NA_UNROLL = 8
_NT =(((1,), (1,)), ((), ()))


def _na_bias_table(rpb):
    kr, kw, w = NA_WIN_ROWS, NA_WIN_COLS, GRID_W
    pat = np.arange(kr)[:, None]
    i = np.arange(kr)[None, :]
    ridx = i - pat + (NA_WIN_ROWS - 1)
    col = np.arange(w)
    cs = np.clip(col - kw // 2, 0, w - kw)
    c2 = np.arange(w)[None, :]
    valid = (c2 >= cs[:, None]) & (c2 < cs[:, None] + kw)
    cidx = np.clip(c2 - col[:, None] + (NA_WIN_COLS - 1), 0, 2 * NA_WIN_COLS - 2)
    row_sel = (ridx[:, :, None] == np.arange(2 * kr - 1)).astype(np.float32)
    col_sel = ((cidx[:, :, None] == np.arange(2 * kw - 1)) & valid[:, :, None]).astype(np.float32)
    by_row = jnp.einsum('pia,hab->hpib', jnp.asarray(row_sel), rpb.astype(F32), precision=lax.Precision.HIGHEST)
    tbl = jnp.einsum('hpib,cdb->hpcid', by_row, jnp.asarray(col_sel), precision=lax.Precision.HIGHEST)
    tbl = tbl + jnp.asarray(np.where(valid, 0.0, NEG_BIG).astype(np.float32))[None, None, :, None, :]
    return tbl.reshape(rpb.shape[0], kr, w, kr * w)


def _na_kernel(q_ref, k_ref, v_ref, bias_ref, o_ref, *, n_lat, n_ctx):
    rows = n_lat // GRID_W
    kr = NA_WIN_ROWS
    scale = HEAD_DIM ** -0.5
    kc = k_ref[pl.ds(n_lat, n_ctx), :]
    vc = v_ref[pl.ds(n_lat, n_ctx), :]

    def attend(s_parts, v_parts):
        m = functools.reduce(jnp.maximum, [jnp.max(s, axis=-1, keepdims=True) for s in s_parts])
        ps = [jnp.exp(s - m) for s in s_parts]
        den = functools.reduce(lambda a, b: a + b, [jnp.sum(p, axis=-1, keepdims=True) for p in ps])
        num = functools.reduce(lambda a, b: a + b,
                               [jnp.dot(p.astype(BF16), v, preferred_element_type=F32) for p, v in zip(ps, v_parts)])
        return num / den

    def body(it, carry):
        rs = [it * NA_UNROLL + n for n in range(NA_UNROLL)]
        r0s = [jnp.clip(r - kr // 2, 0, rows - kr) for r in rs]
        qrows = [pl.ds(pl.multiple_of(r * GRID_W, GRID_W), GRID_W) for r in rs]
        krows = [pl.ds(pl.multiple_of(r0 * GRID_W, GRID_W), kr * GRID_W) for r0 in r0s]
        qs = [q_ref[qr, :] for qr in qrows]
        s_loc = [lax.dot_general(q, k_ref[kr_, :], _NT, preferred_element_type=F32) * scale + bias_ref[0, r - r0]
                 for q, kr_, r, r0 in zip(qs, krows, rs, r0s)]
        s_ctx = [lax.dot_general(q, kc, _NT, preferred_element_type=F32) * scale for q in qs]
        outs = [attend([sl, sc], [v_ref[kr_, :], vc]) for sl, sc, kr_ in zip(s_loc, s_ctx, krows)]
        for qr, o in zip(qrows, outs):
            o_ref[qr, :] = o.astype(o_ref.dtype)
        return carry

    assert rows % NA_UNROLL == 0
    lax.fori_loop(0, rows // NA_UNROLL, body, 0)
    qc = q_ref[pl.ds(n_lat, n_ctx), :]
    s_cc = lax.dot_general(qc, kc, _NT, preferred_element_type=F32) * scale
    o_ref[pl.ds(n_lat, n_ctx), :] = attend([s_cc], [vc]).astype(o_ref.dtype)


def _na_attention(qkv, bias_tbl, *, n_lat):
    r = qkv.shape[0]
    n_ctx = r - n_lat
    assert n_lat // GRID_W >= NA_WIN_ROWS
    blk = lambda off: pl.BlockSpec((r, HEAD_DIM), lambda h: (0, h + off))
    return pl.pallas_call(
        functools.partial(_na_kernel, n_lat=n_lat, n_ctx=n_ctx),
        grid=(N_HEADS,),
        in_specs=[blk(0), blk(N_HEADS), blk(2 * N_HEADS),
                  pl.BlockSpec((1,) + bias_tbl.shape[1:], lambda h: (h, 0, 0, 0))],
        out_specs=pl.BlockSpec((r, HEAD_DIM), lambda h: (0, h)),
        out_shape=jax.ShapeDtypeStruct((r, NA_DIM), BF16),
        compiler_params=_cparams(("arbitrary",)),
        name="na_attention",
    )(qkv, qkv, qkv, bias_tbl)


CONV_ROWS = 256
PAD = SUBLANES


def _rope_tables(n_lat):
    t = np.arange(n_lat)
    n_freq = HEAD_DIM // 4
    inv = ROPE_BASE ** (-np.arange(n_freq, dtype=np.float32) / n_freq)
    ang = np.concatenate([(t // GRID_W).astype(np.float32)[:, None] * inv,
                          (t % GRID_W).astype(np.float32)[:, None] * inv], -1).astype(np.float32)
    ang = jnp.asarray(ang)
    cos = jnp.repeat(jnp.cos(ang), 2, axis=-1)
    sin = jnp.repeat(jnp.sin(ang), 2, axis=-1) * jnp.asarray(np.tile(np.array([-1.0, 1.0], np.float32), HEAD_DIM // 2))
    return cos, sin


def _dn_conv_kernel(x_ref, w_ref, cos_ref, sin_ref, o_ref, xp_ref, *, n_lat, n_ctx, qk):
    half = DN_CONV // 2
    zeros = jnp.zeros((PAD, LANES), F32)
    xp_ref[pl.ds(0, PAD), :] = zeros
    xp_ref[pl.ds(PAD, n_lat), :] = x_ref[pl.ds(0, n_lat), :]
    xp_ref[pl.ds(PAD + n_lat, PAD), :] = zeros
    xp_ref[pl.ds(2 * PAD + n_lat, n_ctx), :] = x_ref[pl.ds(n_lat, n_ctx), :]
    xp_ref[pl.ds(2 * PAD + n_lat + n_ctx, PAD), :] = zeros
    w = w_ref[...]
    lane = lax.broadcasted_iota(jnp.int32, (CONV_ROWS, LANES), 1)
    even = (lane % 2) == 0

    def chunk(src0, dst0, rope_row0):
        big = xp_ref[pl.ds(src0 - PAD, CONV_ROWS + 2 * PAD), :]
        acc = jnp.zeros((CONV_ROWS, LANES), F32)
        for j in range(DN_CONV):
            acc = acc + big[PAD + j - half: PAD + j - half + CONV_ROWS, :] * w[j:j + 1, :]
        y = _silu(acc)
        if qk:
            y = y * lax.rsqrt(jnp.sum(y * y, axis=-1, keepdims=True) + 1e-6)
            if rope_row0 is not None:
                swapped = jnp.where(even, pltpu.roll(y, LANES - 1, 1), pltpu.roll(y, 1, 1))
                y = y * cos_ref[pl.ds(rope_row0, CONV_ROWS), :] + swapped * sin_ref[pl.ds(rope_row0, CONV_ROWS), :]
        o_ref[pl.ds(dst0, CONV_ROWS), :] = y

    def lat_body(i, carry):
        r0 = pl.multiple_of(i * CONV_ROWS, CONV_ROWS)
        chunk(r0 + PAD, r0, r0)
        return carry

    lax.fori_loop(0, n_lat // CONV_ROWS, lat_body, 0)
    for i in range(n_ctx // CONV_ROWS):
        chunk(2 * PAD + n_lat + i * CONV_ROWS, n_lat + i * CONV_ROWS, None)


def _dn_conv(qkv, conv_w, cos, sin, *, n_lat, col0, ncols, qk):
    r = qkv.shape[0]
    n_ctx = r - n_lat
    assert n_lat % CONV_ROWS == 0 and n_ctx % CONV_ROWS == 0
    jb = col0 // LANES
    return pl.pallas_call(
        functools.partial(_dn_conv_kernel, n_lat=n_lat, n_ctx=n_ctx, qk=qk),
        grid=(ncols // LANES,),
        in_specs=[
            pl.BlockSpec((r, LANES), lambda j: (0, j + jb)),
            pl.BlockSpec((DN_CONV, LANES), lambda j: (0, j + jb)),
            pl.BlockSpec((n_lat, LANES), lambda j: (0, 0)),
            pl.BlockSpec((n_lat, LANES), lambda j: (0, 0)),
        ],
        out_specs=pl.BlockSpec((r, LANES), lambda j: (0, j)),
        out_shape=jax.ShapeDtypeStruct((r, ncols), F32),
        scratch_shapes=[pltpu.VMEM((r + 3 * PAD, LANES), F32)],
        compiler_params=_cparams(("arbitrary",)),
        name="dn_conv",
    )(qkv, conv_w, cos, sin)


def _dn_gates_kernel(ab_ref, alog_ref, dtb_ref, o_ref):
    ab = ab_ref[...]
    z = ab + dtb_ref[...]
    softplus = jnp.maximum(z, 0.0) + jnp.log(1.0 + jnp.exp(-jnp.abs(z)))
    lane = lax.broadcasted_iota(jnp.int32, ab.shape, 1)
    g = jnp.where(lane < 2 * N_HEADS, -jnp.exp(alog_ref[...]) * softplus, 0.0)
    n = ab.shape[0]
    i = lax.broadcasted_iota(jnp.int32, (n, n), 0)
    j = lax.broadcasted_iota(jnp.int32, (n, n), 1)
    same = (i // DN_CHUNK) == (j // DN_CHUNK)
    one = lambda m: jnp.where(m, 1.0, 0.0).astype(BF16)
    gc = jnp.where(lane < N_HEADS, _dot_exact_left(one(same & (i >= j)), g), _dot_exact_left(one(same & (i <= j)), g))
    tot = _dot_exact_left(one(same), g)
    quarter = LANES // 4
    o_ref[...] = jnp.where(lane < quarter, g,
                           jnp.where(lane < 2 * quarter, _sigmoid(ab),
                                     jnp.where(lane < 3 * quarter, pltpu.roll(gc, 2 * quarter, 1),
                                               pltpu.roll(tot, 3 * quarter, 1))))


def _dn_gates(ab, a_log, dt_bias):
    r = ab.shape[0]
    pad = lambda v: jnp.zeros((1, LANES), F32).at[0, :2 * N_HEADS].set(v.astype(F32).reshape(-1))
    return pl.pallas_call(
        _dn_gates_kernel,
        grid=(r // ELT_ROWS,),
        in_specs=[pl.BlockSpec((ELT_ROWS, LANES), lambda i: (i, 0)),
                  pl.BlockSpec((1, LANES), lambda i: (0, 0)),
                  pl.BlockSpec((1, LANES), lambda i: (0, 0))],
        out_specs=pl.BlockSpec((ELT_ROWS, LANES), lambda i: (i, 0)),
        out_shape=jax.ShapeDtypeStruct((r, LANES), F32),
        compiler_params=_cparams(("arbitrary",)),
        name="dn_gates",
    )(ab, pad(a_log), pad(dt_bias))


BLK = 2 * DN_CHUNK


def _split3(x):
    x1 = x.astype(BF16)
    r1 = x - x1.astype(F32)
    x2 = r1.astype(BF16)
    x3 = (r1 - x2.astype(F32)).astype(BF16)
    return x1, x2, x3


def _dot(a, b):
    return jnp.dot(a, b, preferred_element_type=F32)


def _dot_exact_left(m01, x):
    x1, x2, x3 = _split3(x)
    return _dot(m01, x1) + _dot(m01, x2) + _dot(m01, x3)


def _dot_hi(a, b):
    a1 = a.astype(BF16)
    a2 = (a - a1.astype(F32)).astype(BF16)
    b1 = b.astype(BF16)
    b2 = (b - b1.astype(F32)).astype(BF16)
    return _dot(a1, b1) + _dot(a1, b2) + _dot(a2, b1)


def _split2(x):
    x1 = x.astype(BF16)
    return x1, (x - x1.astype(F32)).astype(BF16)


def _dn_chunk_math(units):
    n = BLK
    i = lax.broadcasted_iota(jnp.int32, (n, n), 0)
    j = lax.broadcasted_iota(jnp.int32, (n, n), 1)
    same = (i // DN_CHUNK) == (j // DN_CHUNK)
    masks = {False: (same & (i >= j), same & (i > j)), True: (same & (i <= j), same & (i < j))}
    eye = jnp.where(i == j, 1.0, 0.0)
    each = lambda fn, *lists: [fn(*xs) for xs in zip(*lists)]
    q, k, v, gc_col, gc_row, tot_col, beta_col, rev = map(list, zip(*units))
    incl = [masks[r][0] for r in rev]
    strict = [masks[r][1] for r in rev]
    decay = each(lambda m, c, r: jnp.where(m, jnp.exp(jnp.where(m, c - r, 0.0)), 0.0), incl, gc_col, gc_row)
    kb = each(lambda a, b: a * b, k, beta_col)
    qs = [x * (HEAD_DIM ** -0.5) for x in q]
    kk_qk = each(lambda a, b, c: lax.dot_general(jnp.concatenate([a, b], axis=0).astype(BF16), c.astype(BF16), _NT,
                                                 preferred_element_type=F32), kb, qs, k)
    a = each(lambda m, x, dc: jnp.where(m, x[:n] * dc, 0.0), strict, kk_qk, decay)
    qk = each(lambda m, x, dc: jnp.where(m, x[n:] * dc, 0.0), incl, kk_qk, decay)
    t = [eye - x for x in a]
    p = [_dot_hi(x, x) for x in a]
    levels = int(math.log2(DN_CHUNK)) - 1
    for lvl in range(levels):
        last = lvl == levels - 1
        ps = [_split2(x) for x in p]
        ss = [_split2(tt if last else jnp.concatenate([tt, pp], axis=0)) for tt, pp in zip(t, p)]
        prod = [_dot(s1, p1) + _dot(s1, p2) + _dot(s2, p1) for (s1, s2), (p1, p2) in zip(ss, ps)]
        t = [tt + pr[:n] for tt, pr in zip(t, prod)]
        if not last:
            p = [pr[n:] for pr in prod]
    eg = [jnp.exp(c) for c in gc_col]
    ts = [_split2(x) for x in t]
    rhs = each(lambda vv, b, kk, e: jnp.concatenate([(vv * b).astype(BF16), (kk * e).astype(BF16)], axis=1),
               v, beta_col, kb, eg)
    uw = [_dot(t1, r) + _dot(t2, r) for (t1, t2), r in zip(ts, rhs)]
    kdt = each(lambda kk, tc, c: (kk * jnp.exp(tc - c)).T.astype(BF16), k, tot_col, gc_col)
    return [(x[:, :HEAD_DIM], x[:, HEAD_DIM:].astype(BF16), (qq * e).astype(BF16), kt, m.astype(BF16))
            for x, qq, e, kt, m in zip(uw, qs, eg, kdt, qk)]


PREP_BLOCKS = 2


def _dn_prep_kernel(q_ref, k_ref, v_ref, gc_ref, gr_ref, tot_ref, beta_ref,
                    u_ref, w_ref, qd_ref, kdt_ref, qk_ref):
    where = [(d, pl.ds(b * BLK, BLK)) for b in range(PREP_BLOCKS) for d in range(2)]
    units = [(q_ref[rs, :], k_ref[rs, :], v_ref[rs, :], gc_ref[d, 0, rs, :], gr_ref[d, 0, :, rs],
              tot_ref[d, 0, rs, :], beta_ref[d, 0, rs, :], d == 1) for d, rs in where]
    for (d, rs), (u, w, qd, kdt, qk) in zip(where, _dn_chunk_math(units)):
        u_ref[d, 0, rs, :] = u
        w_ref[d, 0, rs, :] = w
        qd_ref[d, 0, rs, :] = qd
        kdt_ref[d, 0, :, rs] = kdt
        qk_ref[d, 0, rs, :] = qk


def _dn_prep(qk, v, gc_col, gc_row, tot_col, beta_col):
    r = v.shape[0]
    rb = PREP_BLOCKS * BLK
    assert r % rb == 0
    head_blk = pl.BlockSpec((rb, HEAD_DIM), lambda h, t: (t, h))
    k_blk = pl.BlockSpec((rb, HEAD_DIM), lambda h, t: (t, h + N_HEADS))
    col_blk = pl.BlockSpec((2, 1, rb, 1), lambda h, t: (0, h, t, 0))
    row_blk = pl.BlockSpec((2, 1, 1, rb), lambda h, t: (0, h, 0, t))
    out_blk = pl.BlockSpec((2, 1, rb, HEAD_DIM), lambda h, t: (0, h, t, 0))
    out_t_blk = pl.BlockSpec((2, 1, HEAD_DIM, rb), lambda h, t: (0, h, 0, t))
    sds = lambda shape, dt: jax.ShapeDtypeStruct((2, N_HEADS) + shape, dt)
    return pl.pallas_call(
        _dn_prep_kernel,
        grid=(N_HEADS, r // rb),
        in_specs=[head_blk, k_blk, head_blk, col_blk, row_blk, col_blk, col_blk],
        out_specs=[out_blk, out_blk, out_blk, out_t_blk, out_blk],
        out_shape=[sds((r, HEAD_DIM), F32), sds((r, HEAD_DIM), BF16), sds((r, HEAD_DIM), BF16),
                   sds((HEAD_DIM, r), BF16), sds((r, HEAD_DIM), BF16)],
        compiler_params=_cparams(("arbitrary", "arbitrary")),
        name="dn_prep",
    )(qk, qk, v, gc_col, gc_row, tot_col, beta_col)


SCAN_ROWS = 256


SCAN_HEADS = 4


def _dn_scan_kernel(u0, u1, w0, w1, qd0, qd1, kdt0, kdt1, qk0, qk1, tot0, tot1, o0, o1, s_ref):
    @pl.when(pl.program_id(1) == 0)
    def _():
        s_ref[...] = jnp.zeros_like(s_ref)

    n_chunks = SCAN_ROWS // DN_CHUNK
    dirs = ((u0, w0, qd0, kdt0, qk0, tot0, o0), (u1, w1, qd1, kdt1, qk1, tot1, o1))
    chains = [(d, hh) for d in range(2) for hh in range(SCAN_HEADS)]
    s = {ch: s_ref[ch[0], ch[1]] for ch in chains}
    for step in range(n_chunks):
        cs = {0: step, 1: n_chunks - 1 - step}
        rs = {d: pl.ds(cs[d] * DN_CHUNK, DN_CHUNK) for d in range(2)}
        ws = {}
        for d, hh in chains:
            w_ref, qd_ref = dirs[d][1], dirs[d][2]
            wq = jnp.concatenate([w_ref[0, hh, rs[d], :], qd_ref[0, hh, rs[d], :]], axis=0)
            ws[d, hh] = _dot(wq, s[d, hh].astype(BF16))
        vnb = {(d, hh): (dirs[d][0][0, hh, rs[d], :] - ws[d, hh][:DN_CHUNK]).astype(BF16) for d, hh in chains}
        for d, hh in chains:
            kdt_ref, qk_ref, tot_ref, o_ref = dirs[d][3:]
            half = (cs[d] % 2) * DN_CHUNK
            qk = qk_ref[0, hh, rs[d], :][:, half:half + DN_CHUNK]
            o_ref[rs[d], hh * HEAD_DIM:(hh + 1) * HEAD_DIM] = ws[d, hh][DN_CHUNK:] + _dot(qk, vnb[d, hh])
            g_last = jnp.exp(tot_ref[0, hh, pl.ds(cs[d] * DN_CHUNK, 1), :])
            s[d, hh] = s[d, hh] * g_last + _dot(kdt_ref[0, hh, :, rs[d]], vnb[d, hh])
    for d, hh in chains:
        s_ref[d, hh] = s[d, hh]


def _dn_scan(u, w, qd, kdt, qk, tot_col, *, n_lat):
    r = u.shape[2]
    nlb = n_lat // SCAN_ROWS
    assert r - n_lat == SCAN_ROWS and N_HEADS % SCAN_HEADS == 0
    rb = (lambda t: jnp.where(t == 0, nlb, t - 1), lambda t: jnp.where(t == 0, nlb, nlb - t))

    def pair(arr, shape, transposed=False):
        specs = []
        for d in range(2):
            if transposed:
                imap = lambda h, t, d=d: (d, h, 0, rb[d](t))
            else:
                imap = lambda h, t, d=d: (d, h, rb[d](t), 0)
            specs.append(pl.BlockSpec((1, SCAN_HEADS) + shape, imap))
        return specs, [arr, arr]

    in_specs, args = [], []
    for arr, shape, tr in ((u, (SCAN_ROWS, HEAD_DIM), False), (w, (SCAN_ROWS, HEAD_DIM), False),
                           (qd, (SCAN_ROWS, HEAD_DIM), False), (kdt, (HEAD_DIM, SCAN_ROWS), True),
                           (qk, (SCAN_ROWS, HEAD_DIM), False), (tot_col, (SCAN_ROWS, 1), False)):
        sp, ar = pair(arr, shape, tr)
        in_specs += sp
        args += ar
    out_specs = [pl.BlockSpec((SCAN_ROWS, SCAN_HEADS * HEAD_DIM), lambda h, t, d=d: (rb[d](t), h)) for d in range(2)]
    return pl.pallas_call(
        _dn_scan_kernel,
        grid=(N_HEADS // SCAN_HEADS, nlb + 1),
        in_specs=in_specs,
        out_specs=out_specs,
        out_shape=[jax.ShapeDtypeStruct((r, DN_DIM), F32)] * 2,
        scratch_shapes=[pltpu.VMEM((2, SCAN_HEADS, HEAD_DIM, HEAD_DIM), F32)],
        compiler_params=_cparams(("arbitrary", "arbitrary")),
        name="dn_scan",
    )(*args)


def _dn_out_kernel(of_ref, ob_ref, gate_ref, ng_ref, y_ref):
    for h in range(N_HEADS):
        cs = slice(h * HEAD_DIM, (h + 1) * HEAD_DIM)
        o = of_ref[:, cs] + ob_ref[:, cs]
        o = o * lax.rsqrt(jnp.mean(o * o, axis=-1, keepdims=True) + 1e-6) * ng_ref[...]
        y_ref[:, cs] = (o * _silu(gate_ref[:, cs])).astype(y_ref.dtype)


def _dn_out(o_f, o_b, gate, norm_g):
    r, n = o_f.shape
    row = pl.BlockSpec((ELT_ROWS, n), lambda i: (i, 0))
    return pl.pallas_call(
        _dn_out_kernel,
        grid=(r // ELT_ROWS,),
        in_specs=[row, row, row, pl.BlockSpec((1, HEAD_DIM), lambda i: (0, 0))],
        out_specs=row,
        out_shape=jax.ShapeDtypeStruct((r, n), BF16),
        compiler_params=_cparams(("arbitrary",)),
        name="dn_out",
    )(o_f, o_b, gate, norm_g.astype(F32).reshape(1, HEAD_DIM))


S5_TG = LANES // S5_GROUP
S5_FOLD = S5_CHUNK * LANES
S5_SW = S5_TG * 2 * S5_STATE
S5_PAIR = 2 * SUBLANES


def _s5_tile_params(lam_re, lam_im, log_step, b_re, b_im, c_re, c_im):
    f = lambda a: a.astype(F32)
    lr, li, br, bi, cr, ci = map(f, (lam_re, lam_im, b_re, b_im, c_re, c_im))
    dt = jnp.exp(f(log_step))[..., None]
    mag, ang = jnp.exp(lr * dt), li * dt
    lbr, lbi = mag * jnp.cos(ang), mag * jnp.sin(ang)
    den = lr * lr + li * li
    zr = ((lbr - 1.0) * lr + lbi * li) / den
    zi = (lbi * lr - (lbr - 1.0) * li) / den
    bbr = zr[..., None] * br - zi[..., None] * bi
    bbi = zr[..., None] * bi + zi[..., None] * br
    g, p = lbr.shape[1], lbr.shape[2]
    nt = g // S5_TG
    tile_rows = lambda a: a.reshape(2, nt, LANES, p)
    per_row = lambda a: tile_rows(jnp.repeat(a, S5_GROUP, axis=1))
    cat = lambda a, b: jnp.concatenate([a, b], axis=-1)
    lrr = cat(per_row(lbr), per_row(lbr))
    lii = cat(-per_row(lbi), per_row(lbi))
    bbc = cat(tile_rows(jnp.swapaxes(bbr, 2, 3)), tile_rows(jnp.swapaxes(bbi, 2, 3)))
    cc = cat(tile_rows(cr), tile_rows(ci))

    def cmul(xr, xi, yr, yi):
        return xr * yr - xi * yi, xr * yi + xi * yr

    ar, ai = lbr, lbi
    for _ in range(int(math.log2(S5_CHUNK))):
        ar, ai = cmul(ar, ai, ar, ai)
    pw = [(ar, ai)]
    for _ in range(SUBLANES - 1):
        pw.append(cmul(*pw[-1], ar, ai))
    rows = lambda xr, xi: jnp.stack([cat(xr, xr), cat(-xi, xi)], axis=2)
    steps = jnp.stack([jnp.broadcast_to(rows(*pw[n - 1])[:, :, :, None, :], (2, g, 2, SUBLANES, 2 * p)) for n in (1, 2, 4)], axis=2)
    fwd = jnp.stack([rows(*pw[j])[0] for j in range(SUBLANES)], axis=2)
    bwd = jnp.stack([rows(*pw[SUBLANES - 1 - j])[1] for j in range(SUBLANES)], axis=2)
    consts = jnp.concatenate([steps, jnp.stack([fwd, bwd])[:, :, None]], axis=2)
    consts = consts.reshape(2, nt, S5_TG, 4, 2, SUBLANES, 2 * p).transpose(1, 0, 3, 4, 5, 2, 6)
    return lrr, lii, bbc, cc, consts.reshape(nt, 2, 4, 2, SUBLANES, S5_SW)


def _cmul_lanes(x, rr, ii):
    return x * rr + pltpu.roll(x, S5_STATE, 1) * ii


def _s5_ops_kernel(lrr_ref, lii_ref, bbc_ref, cc_ref, *out_refs, toeplitz):
    row = lax.broadcasted_iota(jnp.int32, (LANES, LANES), 0)
    lane = lax.broadcasted_iota(jnp.int32, (LANES, LANES), 1)
    same_group = (row // S5_GROUP) == (lane // S5_GROUP)
    conj = jnp.where(lane < S5_STATE, 1.0, -1.0)
    c = S5_CHUNK
    lag = []
    for d in range(2):
        rr, ii = lrr_ref[d, 0], lii_ref[d, 0]
        xs, zs = [bbc_ref[d, 0]], [cc_ref[d, 0]]
        for _ in range(c):
            xs.append(_cmul_lanes(xs[-1], rr, ii))
            zs.append(_cmul_lanes(zs[-1], rr, ii))
        if toeplitz:
            zc = zs[0] * conj
            z1, z2 = _split2(zc)
            ks = []
            for l in range(c):
                x1, x2 = _split2(xs[l])
                k = (lax.dot_general(x1, z1, _NT, preferred_element_type=F32)
                     + lax.dot_general(x1, z2, _NT, preferred_element_type=F32)
                     + lax.dot_general(x2, z1, _NT, preferred_element_type=F32))
                ks.append(jnp.where(same_group, k, 0.0))
            lag.append(ks)
        else:
            bc_ref, cct_ref = out_refs
            for i in range(c):
                xin = xs[c - 1 - i] if d == 0 else xs[i]
                zout = (zs[i + 1] if d == 0 else zs[c - i]) * conj
                rs = slice(i * LANES, (i + 1) * LANES)
                for t in range(S5_TG):
                    sel = (row // S5_GROUP) == t
                    cs = slice(t * LANES, (t + 1) * LANES)
                    bc_ref[d, 0, rs, cs] = jnp.where(sel, xin, 0.0).astype(BF16)
                    cct_ref[d, 0, rs, cs] = jnp.where(sel, zout, 0.0).astype(BF16)
    if toeplitz:
        (m_ref,) = out_refs
        fwd = [k.astype(BF16) for k in lag[0]]
        bwd = [k.astype(BF16) for k in lag[1]]
        diag = (lag[0][0] + lag[1][0]).astype(BF16)
        for i in range(c):
            for j in range(c):
                blk = diag if i == j else (fwd[j - i] if j > i else bwd[i - j])
                m_ref[0, i * LANES:(i + 1) * LANES, j * LANES:(j + 1) * LANES] = blk


def _s5_ops(lrr, lii, bbc, cc):
    nt = lrr.shape[1]
    par = pl.BlockSpec((2, 1, LANES, LANES), lambda q: (0, q, 0, 0))
    common = dict(grid=(nt,), in_specs=[par] * 4, compiler_params=_cparams(("arbitrary",)))
    m = pl.pallas_call(
        functools.partial(_s5_ops_kernel, toeplitz=True),
        out_specs=pl.BlockSpec((1, S5_FOLD, S5_FOLD), lambda q: (q, 0, 0)),
        out_shape=jax.ShapeDtypeStruct((nt, S5_FOLD, S5_FOLD), BF16),
        name="s5_ops_toeplitz", **common)(lrr, lii, bbc, cc)
    st = pl.BlockSpec((2, 1, S5_FOLD, S5_SW), lambda q: (0, q, 0, 0))
    bc, cct = pl.pallas_call(
        functools.partial(_s5_ops_kernel, toeplitz=False),
        out_specs=[st, st],
        out_shape=[jax.ShapeDtypeStruct((2, nt, S5_FOLD, S5_SW), BF16)] * 2,
        name="s5_ops_state", **common)(lrr, lii, bbc, cc)
    return m, bc, cct


def _s5_state_kernel(uf_ref, bc_ref, k_ref, e_ref, vf_ref, vb_ref, *, n_lat_c, n_ctx_c):
    n_all = n_lat_c + n_ctx_c
    u = uf_ref[0]
    vf = _dot(u, bc_ref[0, 0])
    vf_ref[pl.ds(0, n_ctx_c), :] = vf[n_lat_c:]
    vf_ref[pl.ds(n_ctx_c, n_lat_c), :] = vf[:n_lat_c]
    vb_ref[...] = _dot(u, bc_ref[1, 0])
    sub = lax.broadcasted_iota(jnp.int32, (SUBLANES, LANES), 0)

    lanes = [(d, slice(g * LANES, (g + 1) * LANES)) for d in range(2) for g in range(S5_TG)]

    def tile_scans(xs, carry_rows):
        for n, sh in enumerate((1, 2, 4)):
            shifted = [jnp.where(sub >= sh, pltpu.roll(x, sh, 0), 0.0) if d == 0 else
                       jnp.where(sub < SUBLANES - sh, pltpu.roll(x, SUBLANES - sh, 0), 0.0)
                       for x, (d, _) in zip(xs, lanes)]
            xs = [x + _cmul_lanes(s, k_ref[0, d, n, 0, :, cs], k_ref[0, d, n, 1, :, cs])
                  for x, s, (d, cs) in zip(xs, shifted, lanes)]
        cbs = [jnp.broadcast_to(c, (SUBLANES, LANES)) for c in carry_rows]
        xs = [x + _cmul_lanes(cb, k_ref[0, d, 3, 0, :, cs], k_ref[0, d, 3, 1, :, cs])
              for x, cb, (d, cs) in zip(xs, cbs, lanes)]
        return [(jnp.where(sub == 0, cb, pltpu.roll(x, 1, 0)), x[SUBLANES - 1:SUBLANES, :]) if d == 0 else
                (jnp.where(sub == SUBLANES - 1, cb, pltpu.roll(x, SUBLANES - 1, 0)), x[0:1, :])
                for x, cb, (d, _) in zip(xs, cbs, lanes)]

    n_iter = n_all // S5_PAIR
    src = (vf_ref, vb_ref)

    def body(t, carries):
        base = (pl.multiple_of(t * S5_PAIR, S5_PAIR), pl.multiple_of((n_iter - 1 - t) * S5_PAIR, S5_PAIR))
        first = tile_scans([src[d][pl.ds(base[d] + d * SUBLANES, SUBLANES), cs] for d, cs in lanes], carries)
        second = tile_scans([src[d][pl.ds(base[d] + (1 - d) * SUBLANES, SUBLANES), cs] for d, cs in lanes],
                            [c for _, c in first])
        for (d, cs), (e1, _), (e2, _) in zip(lanes, first, second):
            lo, hi = (e1, e2) if d == 0 else (e2, e1)
            e_ref[d, 0, pl.ds(base[d], S5_PAIR), cs] = jnp.concatenate([lo, hi], axis=0).astype(BF16)
        return tuple(c for _, c in second)

    lax.fori_loop(0, n_iter, body, tuple(jnp.zeros((1, LANES), F32) for _ in lanes))


def _s5_out_kernel(uf_ref, m_ref, cct_ref, e_ref, y_ref, *, n_lat_c, n_ctx_c):
    y = _dot(uf_ref[0, pl.ds(0, n_lat_c), :], m_ref[0])
    y = y + lax.dot_general(e_ref[0, 0, pl.ds(n_ctx_c, n_lat_c), :], cct_ref[0, 0], _NT, preferred_element_type=F32)
    y = y + lax.dot_general(e_ref[1, 0, pl.ds(0, n_lat_c), :], cct_ref[1, 0], _NT, preferred_element_type=F32)
    y_ref[0] = y


def _s5_scan(uf, m, bc, cct, consts, *, n_lat):
    nt, n_all, _ = uf.shape
    n_lat_c = n_lat // S5_CHUNK
    n_ctx_c = n_all - n_lat_c
    assert n_lat_c % S5_PAIR == 0 and n_ctx_c % S5_PAIR == 0
    dims = dict(n_lat_c=n_lat_c, n_ctx_c=n_ctx_c)
    e = pl.pallas_call(
        functools.partial(_s5_state_kernel, **dims),
        grid=(nt,),
        in_specs=[
            pl.BlockSpec((1, n_all, S5_FOLD), lambda q: (q, 0, 0)),
            pl.BlockSpec((2, 1, S5_FOLD, S5_SW), lambda q: (0, q, 0, 0)),
            pl.BlockSpec((1, 2, 4, 2, SUBLANES, S5_SW), lambda q: (q, 0, 0, 0, 0, 0)),
        ],
        out_specs=pl.BlockSpec((2, 1, n_all, S5_SW), lambda q: (0, q, 0, 0)),
        out_shape=jax.ShapeDtypeStruct((2, nt, n_all, S5_SW), BF16),
        scratch_shapes=[pltpu.VMEM((n_all, S5_SW), F32), pltpu.VMEM((n_all, S5_SW), F32)],
        compiler_params=_cparams(("arbitrary",)),
        name="s5_state",
    )(uf, bc, consts)
    halves = S5_FOLD // S5_SW
    return pl.pallas_call(
        functools.partial(_s5_out_kernel, **dims),
        grid=(nt, halves),
        in_specs=[
            pl.BlockSpec((1, n_all, S5_FOLD), lambda q, j: (q, 0, 0)),
            pl.BlockSpec((1, S5_FOLD, S5_SW), lambda q, j: (q, 0, j)),
            pl.BlockSpec((2, 1, S5_SW, S5_SW), lambda q, j: (0, q, j, 0)),
            pl.BlockSpec((2, 1, n_all, S5_SW), lambda q, j: (0, q, 0, 0)),
        ],
        out_specs=pl.BlockSpec((1, n_lat_c, S5_SW), lambda q, j: (q, 0, j)),
        out_shape=jax.ShapeDtypeStruct((nt, n_lat_c, S5_FOLD), F32),
        compiler_params=_cparams(("arbitrary", "arbitrary")),
        name="s5_out",
    )(uf, m, cct, e)


def _gelu_skip_kernel(y_ref, u_ref, d_ref, o_ref):
    z = y_ref[...] + d_ref[...] * u_ref[...]
    inner = math.sqrt(2.0 / math.pi) * (z + 0.044715 * (z * z * z))
    o_ref[...] = (0.5 * z * (1.0 + jnp.tanh(inner))).astype(o_ref.dtype)


def _gelu_skip(y, u, d_skip):
    r, d = y.shape
    row = pl.BlockSpec((ELT_ROWS, d), lambda i: (i, 0))
    return pl.pallas_call(
        _gelu_skip_kernel,
        grid=(r // ELT_ROWS,),
        in_specs=[row, row, pl.BlockSpec((1, d), lambda i: (0, 0))],
        out_specs=row,
        out_shape=jax.ShapeDtypeStruct((r, d), BF16),
        compiler_params=_cparams(("arbitrary",)),
        name="gelu_skip",
    )(y, u, d_skip.astype(F32).reshape(1, d))


def _mixer_na_gdn(u, w_in, w_out, rpb, conv_w, a_log, dt_bias, norm_g, *, n_lat):
    r = u.shape[0]
    wb = w_in.astype(BF16)
    n_main = 3 * NA_DIM + 4 * DN_DIM
    qkv_na = _matmul(u, wb, col0=0, ncols=3 * NA_DIM, tn=512, out_dtype=BF16)
    qkv_dn = _matmul(u, wb, col0=3 * NA_DIM, ncols=3 * DN_DIM, tn=512, out_dtype=F32)
    gate = _matmul(u, wb, col0=3 * NA_DIM + 3 * DN_DIM, ncols=DN_DIM, tn=512, out_dtype=F32)
    w_ab = jnp.pad(w_in[:, n_main:], ((0, 0), (0, LANES - 4 * N_HEADS))).astype(BF16)
    ab = _matmul(u, w_ab, col0=0, ncols=LANES, tn=LANES, out_dtype=F32)

    o_na = _na_attention(qkv_na, _na_bias_table(rpb), n_lat=n_lat)

    cos, sin = _rope_tables(n_lat)
    qk = _dn_conv(qkv_dn, conv_w.astype(F32), cos, sin, n_lat=n_lat, col0=0, ncols=2 * DN_DIM, qk=True)
    v = _dn_conv(qkv_dn, conv_w.astype(F32), cos, sin, n_lat=n_lat, col0=2 * DN_DIM, ncols=DN_DIM, qk=False)
    gb = _dn_gates(ab, a_log, dt_bias)
    gb_t = gb.T.reshape(4, 2, N_HEADS, r)
    beta_col = gb_t[1].reshape(2, N_HEADS, r, 1)
    gc_col = gb_t[2].reshape(2, N_HEADS, r, 1)
    gc_row = gb_t[2].reshape(2, N_HEADS, 1, r)
    tot_col = gb_t[3].reshape(2, N_HEADS, r, 1)
    u_c, w_c, qd, kdt, qkm = _dn_prep(qk, v, gc_col, gc_row, tot_col, beta_col)
    o_f, o_b = _dn_scan(u_c, w_c, qd, kdt, qkm, tot_col, n_lat=n_lat)
    o_dn = _dn_out(o_f, o_b, gate, norm_g)
    return _matmul_cat(o_na, o_dn, w_out.astype(BF16), tn=512, out_dtype=BF16)


def _mixer_s5(u_bf, u_f32, lam_re, lam_im, log_step, b_re, b_im, c_re, c_im, d_skip, w_out, w_gate, *, n_lat):
    r, d = u_bf.shape
    nt = d // LANES
    uf = u_bf.reshape(r, nt, LANES).transpose(1, 0, 2).reshape(nt, r // S5_CHUNK, S5_FOLD)
    lrr, lii, bbc, cc, consts = _s5_tile_params(lam_re, lam_im, log_step, b_re, b_im, c_re, c_im)
    m, bc, cct = _s5_ops(lrr, lii, bbc, cc)
    yf = _s5_scan(uf, m, bc, cct, consts, n_lat=n_lat)
    y = yf.reshape(nt, n_lat, LANES).transpose(1, 0, 2).reshape(n_lat, d)
    gz = _gelu_skip(y, u_f32, d_skip)
    return _matmul_pair(gz, w_out.astype(BF16), w_gate.astype(BF16), col1=0, col2=0, ncols=d, tn=512,
                        gate_second=True, out_dtype=BF16)


def kernel(x, c, ctx, c_ctx, w_mod, b_mod, ln_g, ln_b, ffn_w_in, ffn_w_out, ab_w_in, ab_w_out, na_rpb, dn_conv_w, dn_a_log, dn_dt_bias, dn_norm_g, s5_lam_re, s5_lam_im, s5_log_step, s5_b_re, s5_b_im, s5_c_re, s5_c_im, s5_d, s5_w_out, s5_w_gate):
    depth, d = w_mod.shape[0], x.shape[2]
    assert depth == 2 and x.shape[0] == 1, "layer 0 is the NA/DeltaNet layer, layer 1 the final S5 layer"
    n_lat = x.shape[1]
    alpha = (2.0 * depth) ** 0.25
    h = jnp.concatenate([x[0], ctx[0]], axis=0)
    mods = _modulation(c, c_ctx, w_mod, b_mod)
    lng = ln_g.astype(F32).reshape(depth * 3, 1, d)
    lnb = ln_b.astype(F32).reshape(depth * 3, 1, d)
    res_ln = functools.partial(_res_ln, mods=mods, ln_g=lng, ln_b=lnb, alpha=alpha, n_lat=n_lat)
    ffn = lambda uu, l, s: _ffn(uu, ffn_w_in[l, s].astype(BF16), ffn_w_out[l, s].astype(BF16))

    u = _modulate(h, mods, layer=0, sub=0, n_lat=n_lat)
    h, u = res_ln(h, ffn(u, 0, 0), layer=0, sub=0, weight=MACARON_WEIGHT, next_mod=(0, 1))
    y = _mixer_na_gdn(u, ab_w_in[0], ab_w_out[0], na_rpb[0], dn_conv_w[0], dn_a_log[0], dn_dt_bias[0], dn_norm_g[0],
                      n_lat=n_lat)
    h, u = res_ln(h, y, layer=0, sub=1, weight=1.0, next_mod=(0, 2))
    h, u = res_ln(h, ffn(u, 0, 1), layer=0, sub=2, weight=MACARON_WEIGHT, next_mod=(1, 0))

    h, u, u32 = res_ln(h, ffn(u, 1, 0), layer=1, sub=0, weight=MACARON_WEIGHT, next_mod=(1, 1), u_dtypes=(BF16, F32))
    y = _mixer_s5(u, u32, s5_lam_re[0], s5_lam_im[0], s5_log_step[0], s5_b_re[0], s5_b_im[0], s5_c_re[0], s5_c_im[0],
                  s5_d[0], s5_w_out[0], s5_w_gate[0], n_lat=n_lat)
    h, u = res_ln(h, y, layer=1, sub=1, weight=1.0, next_mod=(1, 2))
    (out,) = res_ln(h, ffn(u, 1, 1), layer=1, sub=2, weight=MACARON_WEIGHT, next_mod=None)
    return out[None]
```

```python
import functools
import math

import numpy as np
import jax
import jax.numpy as jnp
from jax import lax
from jax.experimental import pallas as pl
from jax.experimental.pallas import tpu as pltpu

GRID_W = 64
HEAD_DIM = 128
N_HEADS = 16
NA_DIM = N_HEADS * HEAD_DIM
DN_DIM = N_HEADS * HEAD_DIM
NA_WIN_ROWS = 8
NA_WIN_COLS = 16
DN_CONV = 5
DN_CHUNK = 64
ROPE_BASE = 10000.0
S5_GROUP = 16
S5_STATE = 64
S5_CHUNK = 16
N_MOD = 9
LN_EPS = 1e-5
MACARON_WEIGHT = 0.5

V7X_VMEM_BYTES = 64 * 1024 * 1024
VMEM_LIMIT = V7X_VMEM_BYTES - 8 * 1024 * 1024
LANES = 128
SUBLANES = 8

F32 = jnp.float32
BF16 = jnp.bfloat16


def _cparams(sem):
    return pltpu.CompilerParams(dimension_semantics=sem, vmem_limit_bytes=VMEM_LIMIT)


def _silu(x):
    return x * (1.0 / (1.0 + jnp.exp(-x)))


def _sigmoid(x):
    return 1.0 / (1.0 + jnp.exp(-x))


def _row_tile(rows, candidates=(768, 512, 256)):
    for t in candidates:
        if rows % t == 0:
            return t
    raise ValueError(f"unsupported row count {rows}")


F32_WEIGHT_ROW_TILES = (1408, 1024, 768, 512, 256)


def _weight_row_tile(rows, w):
    return _row_tile(rows, F32_WEIGHT_ROW_TILES if w.dtype == F32 else (768, 512, 256))


def _weight_spec(lead, k, tn, col_block0, n_grid):
    if n_grid == 2:
        imap = lambda i, j: lead + (0, j + col_block0)
    else:
        imap = lambda i, j, kk: lead + (kk, j + col_block0)
    return pl.BlockSpec((None,) * len(lead) + (k, tn), imap)


def _mod_kernel(c_ref, w_ref, b_ref, o_ref):
    s = _silu(c_ref[...]).astype(BF16)
    w = w_ref[0].astype(BF16)
    o_ref[0] = jnp.dot(s, w, preferred_element_type=F32) + b_ref[0]


def _modulation(c, c_ctx, w_mod, b_mod):
    depth, d, n = w_mod.shape
    tn = 512
    cc = jnp.zeros((SUBLANES, d), F32).at[0].set(c[0]).at[1].set(c_ctx)
    out = pl.pallas_call(
        _mod_kernel,
        grid=(depth, n // tn),
        in_specs=[
            pl.BlockSpec((SUBLANES, d), lambda l, j: (0, 0)),
            pl.BlockSpec((1, d, tn), lambda l, j: (l, 0, j)),
            pl.BlockSpec((1, 1, tn), lambda l, j: (l, 0, j)),
        ],
        out_specs=pl.BlockSpec((1, SUBLANES, tn), lambda l, j: (l, 0, j)),
        out_shape=jax.ShapeDtypeStruct((depth, SUBLANES, n), F32),
        compiler_params=_cparams(("arbitrary", "arbitrary")),
        name="modulation",
    )(cc, w_mod, b_mod.reshape(depth, 1, n))
    return out[:, :2].reshape(depth * 2 * N_MOD, 1, d)


def _mm_kernel(a_ref, w_ref, o_ref):
    o_ref[...] = jnp.dot(a_ref[...], w_ref[...].astype(BF16), preferred_element_type=F32).astype(o_ref.dtype)


def _matmul(a, w, *, lead=(), col0, ncols, tn, out_dtype):
    r, k = a.shape
    tm = _weight_row_tile(r, w)
    assert col0 % tn == 0 and ncols % tn == 0 and w.shape[len(lead)] == k
    return pl.pallas_call(
        _mm_kernel,
        grid=(r // tm, ncols // tn),
        in_specs=[pl.BlockSpec((tm, k), lambda i, j: (i, 0)), _weight_spec(lead, k, tn, col0 // tn, 2)],
        out_specs=pl.BlockSpec((tm, tn), lambda i, j: (i, j)),
        out_shape=jax.ShapeDtypeStruct((r, ncols), out_dtype),
        compiler_params=_cparams(("arbitrary", "arbitrary")),
        name="matmul",
    )(a, w)


def _mm_pair_kernel(a_ref, w1_ref, w2_ref, o_ref, *, gate_second):
    a = a_ref[...]
    p1 = jnp.dot(a, w1_ref[...].astype(BF16), preferred_element_type=F32)
    p2 = jnp.dot(a, w2_ref[...].astype(BF16), preferred_element_type=F32)
    if gate_second:
        o = p1 * _sigmoid(p2)
    else:
        o = _silu(p1) * p2
    o_ref[...] = o.astype(o_ref.dtype)


def _matmul_pair(a, w1, w2, *, lead=(), col1, col2, ncols, tn, gate_second, out_dtype):
    r, k = a.shape
    tm = _weight_row_tile(r, w1)
    assert col1 % tn == 0 and col2 % tn == 0 and ncols % tn == 0
    return pl.pallas_call(
        functools.partial(_mm_pair_kernel, gate_second=gate_second),
        grid=(r // tm, ncols // tn),
        in_specs=[pl.BlockSpec((tm, k), lambda i, j: (i, 0)),
                  _weight_spec(lead, k, tn, col1 // tn, 2), _weight_spec(lead, k, tn, col2 // tn, 2)],
        out_specs=pl.BlockSpec((tm, tn), lambda i, j: (i, j)),
        out_shape=jax.ShapeDtypeStruct((r, ncols), out_dtype),
        compiler_params=_cparams(("arbitrary", "arbitrary")),
        name="matmul_pair",
    )(a, w1, w2)


def _mm_cat_kernel(a1_ref, a2_ref, w_ref, o_ref, acc_ref):
    k = pl.program_id(2)

    @pl.when(k == 0)
    def _():
        acc_ref[...] = jnp.dot(a1_ref[...], w_ref[...].astype(BF16), preferred_element_type=F32)

    @pl.when(k == 1)
    def _():
        o_ref[...] = (acc_ref[...]
                      + jnp.dot(a2_ref[...], w_ref[...].astype(BF16), preferred_element_type=F32)).astype(o_ref.dtype)


def _matmul_cat(a1, a2, w, *, tn, out_dtype):
    r, k1 = a1.shape
    assert a2.shape == (r, k1) and w.shape[0] == 2 * k1
    n = w.shape[1]
    tm = _weight_row_tile(r, w)
    return pl.pallas_call(
        _mm_cat_kernel,
        grid=(r // tm, n // tn, 2),
        in_specs=[
            pl.BlockSpec((tm, k1), lambda i, j, k: (i, 0)),
            pl.BlockSpec((tm, k1), lambda i, j, k: (i, 0)),
            _weight_spec((), k1, tn, 0, 3),
        ],
        out_specs=pl.BlockSpec((tm, tn), lambda i, j, k: (i, j)),
        out_shape=jax.ShapeDtypeStruct((r, n), out_dtype),
        scratch_shapes=[pltpu.VMEM((tm, tn), F32)],
        compiler_params=_cparams(("arbitrary", "arbitrary", "arbitrary")),
        name="matmul_cat",
    )(a1, a2, w)


ELT_ROWS = 256


def _modulate_kernel(z_ref, shift_ref, scale_ref, u_ref):
    u_ref[...] = (z_ref[...] * (1.0 + scale_ref[0]) + shift_ref[0]).astype(u_ref.dtype)


def _mod_row(layer, r, n_lat_tiles):
    return lambda i: ((layer * 2 + jnp.where(i >= n_lat_tiles, 1, 0)) * N_MOD + r, 0, 0)


def _modulate(z, mods, *, layer, sub, n_lat):
    r, d = z.shape
    nlt = n_lat // ELT_ROWS
    vec = lambda row: pl.BlockSpec((1, 1, d), _mod_row(layer, row, nlt))
    return pl.pallas_call(
        _modulate_kernel,
        grid=(r // ELT_ROWS,),
        in_specs=[pl.BlockSpec((ELT_ROWS, d), lambda i: (i, 0)), vec(3 * sub), vec(3 * sub + 1)],
        out_specs=pl.BlockSpec((ELT_ROWS, d), lambda i: (i, 0)),
        out_shape=jax.ShapeDtypeStruct((r, d), BF16),
        compiler_params=_cparams(("arbitrary",)),
        name="modulate",
    )(z, mods, mods)


def _res_ln_kernel(*refs, alpha, weight, n_u):
    z_ref, y_ref, gate_ref, g_ref, b_ref = refs[:5]
    t = alpha * z_ref[...] + (weight * gate_ref[0]) * y_ref[...].astype(F32)
    mu = jnp.mean(t, axis=-1, keepdims=True)
    tc = t - mu
    var = jnp.mean(tc * tc, axis=-1, keepdims=True)
    x = tc * lax.rsqrt(var + LN_EPS) * g_ref[0] + b_ref[0]
    if n_u == 0:
        refs[5][...] = x
        return
    shift_ref, scale_ref, x_ref = refs[5], refs[6], refs[7]
    x_ref[...] = x
    u = x * (1.0 + scale_ref[0]) + shift_ref[0]
    for u_ref in refs[8:]:
        u_ref[...] = u.astype(u_ref.dtype)


def _res_ln(z, y, *, mods, ln_g, ln_b, layer, sub, weight, alpha, n_lat, next_mod, u_dtypes=(BF16,)):
    r, d = y.shape
    nlt = n_lat // ELT_ROWS
    vec = lambda lyr, row: pl.BlockSpec((1, 1, d), _mod_row(lyr, row, nlt))
    ln_idx = layer * 3 + sub
    ln_spec = pl.BlockSpec((1, 1, d), lambda i: (ln_idx, 0, 0))
    row_spec = pl.BlockSpec((ELT_ROWS, d), lambda i: (i, 0))
    in_specs = [row_spec, row_spec, vec(layer, 3 * sub + 2), ln_spec, ln_spec]
    args = [z, y, mods, ln_g, ln_b]
    out_specs = [row_spec]
    out_shape = [jax.ShapeDtypeStruct((r, d), F32)]
    n_u = 0
    if next_mod is not None:
        nl, ns = next_mod
        in_specs += [vec(nl, 3 * ns), vec(nl, 3 * ns + 1)]
        args += [mods, mods]
        n_u = len(u_dtypes)
        out_specs += [row_spec] * n_u
        out_shape += [jax.ShapeDtypeStruct((r, d), dt) for dt in u_dtypes]
    res = pl.pallas_call(
        functools.partial(_res_ln_kernel, alpha=alpha, weight=weight, n_u=n_u),
        grid=(r // ELT_ROWS,),
        in_specs=in_specs,
        out_specs=out_specs,
        out_shape=out_shape,
        compiler_params=_cparams(("arbitrary",)),
        name="res_ln",
    )(*args)
    return res


def _ffn(u, w_in, w_out, lead):
    d_ff, d = w_out.shape[-2:]
    hmid = _matmul_pair(u, w_in, w_in, lead=lead, col1=0, col2=d_ff, ncols=d_ff, tn=256, gate_second=False,
                        out_dtype=BF16)
    return _matmul(hmid, w_out, lead=lead, col0=0, ncols=d, tn=512, out_dtype=BF16)


NEG_BIG = -1e30
NA_UNROLL = 8
_NT =(((1,), (1,)), ((), ()))


def _na_bias_table(rpb):
    kr, kw, w = NA_WIN_ROWS, NA_WIN_COLS, GRID_W
    pat = np.arange(kr)[:, None]
    i = np.arange(kr)[None, :]
    ridx = i - pat + (NA_WIN_ROWS - 1)
    col = np.arange(w)
    cs = np.clip(col - kw // 2, 0, w - kw)
    c2 = np.arange(w)[None, :]
    valid = (c2 >= cs[:, None]) & (c2 < cs[:, None] + kw)
    cidx = np.clip(c2 - col[:, None] + (NA_WIN_COLS - 1), 0, 2 * NA_WIN_COLS - 2)
    row_sel = (ridx[:, :, None] == np.arange(2 * kr - 1)).astype(np.float32)
    col_sel = ((cidx[:, :, None] == np.arange(2 * kw - 1)) & valid[:, :, None]).astype(np.float32)
    by_row = jnp.einsum('pia,hab->hpib', jnp.asarray(row_sel), rpb.astype(F32), precision=lax.Precision.HIGHEST)
    tbl = jnp.einsum('hpib,cdb->hpcid', by_row, jnp.asarray(col_sel), precision=lax.Precision.HIGHEST)
    tbl = tbl + jnp.asarray(np.where(valid, 0.0, NEG_BIG).astype(np.float32))[None, None, :, None, :]
    return tbl.reshape(rpb.shape[0], kr, w, kr * w)


def _na_kernel(q_ref, k_ref, v_ref, bias_ref, o_ref, *, n_lat, n_ctx):
    rows = n_lat // GRID_W
    kr = NA_WIN_ROWS
    scale = HEAD_DIM ** -0.5
    kc = k_ref[pl.ds(n_lat, n_ctx), :]
    vc = v_ref[pl.ds(n_lat, n_ctx), :]

    def attend(s_parts, v_parts):
        m = functools.reduce(jnp.maximum, [jnp.max(s, axis=-1, keepdims=True) for s in s_parts])
        ps = [jnp.exp(s - m) for s in s_parts]
        den = functools.reduce(lambda a, b: a + b, [jnp.sum(p, axis=-1, keepdims=True) for p in ps])
        num = functools.reduce(lambda a, b: a + b,
                               [jnp.dot(p.astype(BF16), v, preferred_element_type=F32) for p, v in zip(ps, v_parts)])
        return num / den

    def body(it, carry):
        rs = [it * NA_UNROLL + n for n in range(NA_UNROLL)]
        r0s = [jnp.clip(r - kr // 2, 0, rows - kr) for r in rs]
        qrows = [pl.ds(pl.multiple_of(r * GRID_W, GRID_W), GRID_W) for r in rs]
        krows = [pl.ds(pl.multiple_of(r0 * GRID_W, GRID_W), kr * GRID_W) for r0 in r0s]
        qs = [q_ref[qr, :] for qr in qrows]
        s_loc = [lax.dot_general(q, k_ref[kr_, :], _NT, preferred_element_type=F32) * scale + bias_ref[0, r - r0]
                 for q, kr_, r, r0 in zip(qs, krows, rs, r0s)]
        s_ctx = [lax.dot_general(q, kc, _NT, preferred_element_type=F32) * scale for q in qs]
        outs = [attend([sl, sc], [v_ref[kr_, :], vc]) for sl, sc, kr_ in zip(s_loc, s_ctx, krows)]
        for qr, o in zip(qrows, outs):
            o_ref[qr, :] = o.astype(o_ref.dtype)
        return carry

    assert rows % NA_UNROLL == 0
    lax.fori_loop(0, rows // NA_UNROLL, body, 0)
    qc = q_ref[pl.ds(n_lat, n_ctx), :]
    s_cc = lax.dot_general(qc, kc, _NT, preferred_element_type=F32) * scale
    o_ref[pl.ds(n_lat, n_ctx), :] = attend([s_cc], [vc]).astype(o_ref.dtype)


def _na_attention(qkv, bias_tbl, *, n_lat):
    r = qkv.shape[0]
    n_ctx = r - n_lat
    assert n_lat // GRID_W >= NA_WIN_ROWS
    blk = lambda off: pl.BlockSpec((r, HEAD_DIM), lambda h: (0, h + off))
    return pl.pallas_call(
        functools.partial(_na_kernel, n_lat=n_lat, n_ctx=n_ctx),
        grid=(N_HEADS,),
        in_specs=[blk(0), blk(N_HEADS), blk(2 * N_HEADS),
                  pl.BlockSpec((1,) + bias_tbl.shape[1:], lambda h: (h, 0, 0, 0))],
        out_specs=pl.BlockSpec((r, HEAD_DIM), lambda h: (0, h)),
        out_shape=jax.ShapeDtypeStruct((r, NA_DIM), BF16),
        compiler_params=_cparams(("arbitrary",)),
        name="na_attention",
    )(qkv, qkv, qkv, bias_tbl)


CONV_ROWS = 256
PAD = SUBLANES


def _rope_tables(n_lat):
    t = np.arange(n_lat)
    n_freq = HEAD_DIM // 4
    inv = ROPE_BASE ** (-np.arange(n_freq, dtype=np.float32) / n_freq)
    ang = np.concatenate([(t // GRID_W).astype(np.float32)[:, None] * inv,
                          (t % GRID_W).astype(np.float32)[:, None] * inv], -1).astype(np.float32)
    ang = jnp.asarray(ang)
    cos = jnp.repeat(jnp.cos(ang), 2, axis=-1)
    sin = jnp.repeat(jnp.sin(ang), 2, axis=-1) * jnp.asarray(np.tile(np.array([-1.0, 1.0], np.float32), HEAD_DIM // 2))
    return cos, sin


def _dn_conv_kernel(x_ref, w_ref, cos_ref, sin_ref, o_ref, xp_ref, *, n_lat, n_ctx, qk):
    half = DN_CONV // 2
    zeros = jnp.zeros((PAD, LANES), F32)
    xp_ref[pl.ds(0, PAD), :] = zeros
    xp_ref[pl.ds(PAD, n_lat), :] = x_ref[pl.ds(0, n_lat), :]
    xp_ref[pl.ds(PAD + n_lat, PAD), :] = zeros
    xp_ref[pl.ds(2 * PAD + n_lat, n_ctx), :] = x_ref[pl.ds(n_lat, n_ctx), :]
    xp_ref[pl.ds(2 * PAD + n_lat + n_ctx, PAD), :] = zeros
    w = w_ref[...]
    lane = lax.broadcasted_iota(jnp.int32, (CONV_ROWS, LANES), 1)
    even = (lane % 2) == 0

    def chunk(src0, dst0, rope_row0):
        big = xp_ref[pl.ds(src0 - PAD, CONV_ROWS + 2 * PAD), :]
        acc = jnp.zeros((CONV_ROWS, LANES), F32)
        for j in range(DN_CONV):
            acc = acc + big[PAD + j - half: PAD + j - half + CONV_ROWS, :] * w[j:j + 1, :]
        y = _silu(acc)
        if qk:
            y = y * lax.rsqrt(jnp.sum(y * y, axis=-1, keepdims=True) + 1e-6)
            if rope_row0 is not None:
                swapped = jnp.where(even, pltpu.roll(y, LANES - 1, 1), pltpu.roll(y, 1, 1))
                y = y * cos_ref[pl.ds(rope_row0, CONV_ROWS), :] + swapped * sin_ref[pl.ds(rope_row0, CONV_ROWS), :]
        o_ref[pl.ds(dst0, CONV_ROWS), :] = y

    def lat_body(i, carry):
        r0 = pl.multiple_of(i * CONV_ROWS, CONV_ROWS)
        chunk(r0 + PAD, r0, r0)
        return carry

    lax.fori_loop(0, n_lat // CONV_ROWS, lat_body, 0)
    for i in range(n_ctx // CONV_ROWS):
        chunk(2 * PAD + n_lat + i * CONV_ROWS, n_lat + i * CONV_ROWS, None)


def _dn_conv(qkv, conv_w, cos, sin, *, n_lat, col0, ncols, qk):
    r = qkv.shape[0]
    n_ctx = r - n_lat
    assert n_lat % CONV_ROWS == 0 and n_ctx % CONV_ROWS == 0
    jb = col0 // LANES
    return pl.pallas_call(
        functools.partial(_dn_conv_kernel, n_lat=n_lat, n_ctx=n_ctx, qk=qk),
        grid=(ncols // LANES,),
        in_specs=[
            pl.BlockSpec((r, LANES), lambda j: (0, j + jb)),
            pl.BlockSpec((DN_CONV, LANES), lambda j: (0, j + jb)),
            pl.BlockSpec((n_lat, LANES), lambda j: (0, 0)),
            pl.BlockSpec((n_lat, LANES), lambda j: (0, 0)),
        ],
        out_specs=pl.BlockSpec((r, LANES), lambda j: (0, j)),
        out_shape=jax.ShapeDtypeStruct((r, ncols), F32),
        scratch_shapes=[pltpu.VMEM((r + 3 * PAD, LANES), F32)],
        compiler_params=_cparams(("arbitrary",)),
        name="dn_conv",
    )(qkv, conv_w, cos, sin)


def _dn_gates_kernel(ab_ref, alog_ref, dtb_ref, o_ref):
    ab = ab_ref[...]
    z = ab + dtb_ref[...]
    softplus = jnp.maximum(z, 0.0) + jnp.log(1.0 + jnp.exp(-jnp.abs(z)))
    lane = lax.broadcasted_iota(jnp.int32, ab.shape, 1)
    g = jnp.where(lane < 2 * N_HEADS, -jnp.exp(alog_ref[...]) * softplus, 0.0)
    n = ab.shape[0]
    i = lax.broadcasted_iota(jnp.int32, (n, n), 0)
    j = lax.broadcasted_iota(jnp.int32, (n, n), 1)
    same = (i // DN_CHUNK) == (j // DN_CHUNK)
    one = lambda m: jnp.where(m, 1.0, 0.0).astype(BF16)
    gc = jnp.where(lane < N_HEADS, _dot_exact_left(one(same & (i >= j)), g), _dot_exact_left(one(same & (i <= j)), g))
    tot = _dot_exact_left(one(same), g)
    quarter = LANES // 4
    o_ref[...] = jnp.where(lane < quarter, g,
                           jnp.where(lane < 2 * quarter, _sigmoid(ab),
                                     jnp.where(lane < 3 * quarter, pltpu.roll(gc, 2 * quarter, 1),
                                               pltpu.roll(tot, 3 * quarter, 1))))


def _dn_gates(ab, a_log, dt_bias):
    r = ab.shape[0]
    pad = lambda v: jnp.zeros((1, LANES), F32).at[0, :2 * N_HEADS].set(v.astype(F32).reshape(-1))
    return pl.pallas_call(
        _dn_gates_kernel,
        grid=(r // ELT_ROWS,),
        in_specs=[pl.BlockSpec((ELT_ROWS, LANES), lambda i: (i, 0)),
                  pl.BlockSpec((1, LANES), lambda i: (0, 0)),
                  pl.BlockSpec((1, LANES), lambda i: (0, 0))],
        out_specs=pl.BlockSpec((ELT_ROWS, LANES), lambda i: (i, 0)),
        out_shape=jax.ShapeDtypeStruct((r, LANES), F32),
        compiler_params=_cparams(("arbitrary",)),
        name="dn_gates",
    )(ab, pad(a_log), pad(dt_bias))


BLK = 2 * DN_CHUNK


def _split3(x):
    x1 = x.astype(BF16)
    r1 = x - x1.astype(F32)
    x2 = r1.astype(BF16)
    x3 = (r1 - x2.astype(F32)).astype(BF16)
    return x1, x2, x3


def _dot(a, b):
    return jnp.dot(a, b, preferred_element_type=F32)


def _dot_exact_left(m01, x):
    x1, x2, x3 = _split3(x)
    return _dot(m01, x1) + _dot(m01, x2) + _dot(m01, x3)


def _split2(x):
    x1 = x.astype(BF16)
    return x1, (x - x1.astype(F32)).astype(BF16)


def _dn_chunk_math(units):
    n = BLK
    i = lax.broadcasted_iota(jnp.int32, (n, n), 0)
    j = lax.broadcasted_iota(jnp.int32, (n, n), 1)
    same = (i // DN_CHUNK) == (j // DN_CHUNK)
    masks = {False: (same & (i >= j), same & (i > j)), True: (same & (i <= j), same & (i < j))}
    eye = jnp.where(i == j, 1.0, 0.0)
    each = lambda fn, *lists: [fn(*xs) for xs in zip(*lists)]
    q, k, v, gc_col, gc_row, tot_col, beta_col, rev = map(list, zip(*units))
    incl = [masks[r][0] for r in rev]
    strict = [masks[r][1] for r in rev]
    decay = each(lambda m, c, r: jnp.where(m, jnp.exp(jnp.where(m, c - r, 0.0)), 0.0), incl, gc_col, gc_row)
    kb = each(lambda a, b: a * b, k, beta_col)
    qs = [x * (HEAD_DIM ** -0.5) for x in q]
    kk_qk = each(lambda a, b, c: lax.dot_general(jnp.concatenate([a, b], axis=0).astype(BF16), c.astype(BF16), _NT,
                                                 preferred_element_type=F32), kb, qs, k)
    a = each(lambda m, x, dc: jnp.where(m, x[:n] * dc, 0.0), strict, kk_qk, decay)
    qk = each(lambda m, x, dc: jnp.where(m, x[n:] * dc, 0.0), incl, kk_qk, decay)
    t = [eye - x for x in a]
    ab = [x.astype(BF16) for x in a]
    p = [_dot(x, x) for x in ab]
    levels = int(math.log2(DN_CHUNK)) - 1
    for lvl in range(levels):
        last = lvl == levels - 1
        prod = [_dot((tt if last else jnp.concatenate([tt, pp], axis=0)).astype(BF16), pp.astype(BF16))
                for tt, pp in zip(t, p)]
        t = [tt + pr[:n] for tt, pr in zip(t, prod)]
        if not last:
            p = [pr[n:] for pr in prod]
    eg = [jnp.exp(c) for c in gc_col]
    rhs = each(lambda vv, b, kk, e: jnp.concatenate([(vv * b).astype(BF16), (kk * e).astype(BF16)], axis=1),
               v, beta_col, kb, eg)
    uw = [_dot(tt.astype(BF16), r) for tt, r in zip(t, rhs)]
    kdt = each(lambda kk, tc, c: (kk * jnp.exp(tc - c)).T.astype(BF16), k, tot_col, gc_col)
    return [(x[:, :HEAD_DIM], x[:, HEAD_DIM:].astype(BF16), (qq * e).astype(BF16), kt, m.astype(BF16))
            for x, qq, e, kt, m in zip(uw, qs, eg, kdt, qk)]


PREP_BLOCKS = 2


def _dn_prep_kernel(q_ref, k_ref, v_ref, gc_ref, gr_ref, tot_ref, beta_ref,
                    u_ref, w_ref, qd_ref, kdt_ref, qk_ref):
    where = [(d, pl.ds(b * BLK, BLK)) for b in range(PREP_BLOCKS) for d in range(2)]
    units = [(q_ref[rs, :], k_ref[rs, :], v_ref[rs, :], gc_ref[d, 0, rs, :], gr_ref[d, 0, :, rs],
              tot_ref[d, 0, rs, :], beta_ref[d, 0, rs, :], d == 1) for d, rs in where]
    for (d, rs), (u, w, qd, kdt, qk) in zip(where, _dn_chunk_math(units)):
        u_ref[d, 0, rs, :] = u
        w_ref[d, 0, rs, :] = w
        qd_ref[d, 0, rs, :] = qd
        kdt_ref[d, 0, :, rs] = kdt
        qk_ref[d, 0, rs, :] = qk


def _dn_prep(qk, v, gc_col, gc_row, tot_col, beta_col):
    r = v.shape[0]
    rb = PREP_BLOCKS * BLK
    assert r % rb == 0
    head_blk = pl.BlockSpec((rb, HEAD_DIM), lambda h, t: (t, h))
    k_blk = pl.BlockSpec((rb, HEAD_DIM), lambda h, t: (t, h + N_HEADS))
    col_blk = pl.BlockSpec((2, 1, rb, 1), lambda h, t: (0, h, t, 0))
    row_blk = pl.BlockSpec((2, 1, 1, rb), lambda h, t: (0, h, 0, t))
    out_blk = pl.BlockSpec((2, 1, rb, HEAD_DIM), lambda h, t: (0, h, t, 0))
    out_t_blk = pl.BlockSpec((2, 1, HEAD_DIM, rb), lambda h, t: (0, h, 0, t))
    sds = lambda shape, dt: jax.ShapeDtypeStruct((2, N_HEADS) + shape, dt)
    return pl.pallas_call(
        _dn_prep_kernel,
        grid=(N_HEADS, r // rb),
        in_specs=[head_blk, k_blk, head_blk, col_blk, row_blk, col_blk, col_blk],
        out_specs=[out_blk, out_blk, out_blk, out_t_blk, out_blk],
        out_shape=[sds((r, HEAD_DIM), F32), sds((r, HEAD_DIM), BF16), sds((r, HEAD_DIM), BF16),
                   sds((HEAD_DIM, r), BF16), sds((r, HEAD_DIM), BF16)],
        compiler_params=_cparams(("arbitrary", "arbitrary")),
        name="dn_prep",
    )(qk, qk, v, gc_col, gc_row, tot_col, beta_col)


SCAN_ROWS = 256


SCAN_HEADS = 4


def _dn_scan_kernel(u0, u1, w0, w1, qd0, qd1, kdt0, kdt1, qk0, qk1, tot0, tot1, o0, o1, s_ref):
    @pl.when(pl.program_id(1) == 0)
    def _():
        s_ref[...] = jnp.zeros_like(s_ref)

    n_chunks = SCAN_ROWS // DN_CHUNK
    dirs = ((u0, w0, qd0, kdt0, qk0, tot0, o0), (u1, w1, qd1, kdt1, qk1, tot1, o1))
    chains = [(d, hh) for d in range(2) for hh in range(SCAN_HEADS)]
    s = {ch: s_ref[ch[0], ch[1]] for ch in chains}
    for step in range(n_chunks):
        cs = {0: step, 1: n_chunks - 1 - step}
        rs = {d: pl.ds(cs[d] * DN_CHUNK, DN_CHUNK) for d in range(2)}
        ws = {}
        for d, hh in chains:
            w_ref, qd_ref = dirs[d][1], dirs[d][2]
            wq = jnp.concatenate([w_ref[0, hh, rs[d], :], qd_ref[0, hh, rs[d], :]], axis=0)
            ws[d, hh] = _dot(wq, s[d, hh].astype(BF16))
        vnb = {(d, hh): (dirs[d][0][0, hh, rs[d], :] - ws[d, hh][:DN_CHUNK]).astype(BF16) for d, hh in chains}
        for d, hh in chains:
            kdt_ref, qk_ref, tot_ref, o_ref = dirs[d][3:]
            half = (cs[d] % 2) * DN_CHUNK
            qk = qk_ref[0, hh, rs[d], :][:, half:half + DN_CHUNK]
            o_ref[rs[d], hh * HEAD_DIM:(hh + 1) * HEAD_DIM] = ws[d, hh][DN_CHUNK:] + _dot(qk, vnb[d, hh])
            g_last = jnp.exp(tot_ref[0, hh, pl.ds(cs[d] * DN_CHUNK, 1), :])
            s[d, hh] = s[d, hh] * g_last + _dot(kdt_ref[0, hh, :, rs[d]], vnb[d, hh])
    for d, hh in chains:
        s_ref[d, hh] = s[d, hh]


def _dn_scan(u, w, qd, kdt, qk, tot_col, *, n_lat):
    r = u.shape[2]
    nlb = n_lat // SCAN_ROWS
    assert r - n_lat == SCAN_ROWS and N_HEADS % SCAN_HEADS == 0
    rb = (lambda t: jnp.where(t == 0, nlb, t - 1), lambda t: jnp.where(t == 0, nlb, nlb - t))

    def pair(arr, shape, transposed=False):
        specs = []
        for d in range(2):
            if transposed:
                imap = lambda h, t, d=d: (d, h, 0, rb[d](t))
            else:
                imap = lambda h, t, d=d: (d, h, rb[d](t), 0)
            specs.append(pl.BlockSpec((1, SCAN_HEADS) + shape, imap))
        return specs, [arr, arr]

    in_specs, args = [], []
    for arr, shape, tr in ((u, (SCAN_ROWS, HEAD_DIM), False), (w, (SCAN_ROWS, HEAD_DIM), False),
                           (qd, (SCAN_ROWS, HEAD_DIM), False), (kdt, (HEAD_DIM, SCAN_ROWS), True),
                           (qk, (SCAN_ROWS, HEAD_DIM), False), (tot_col, (SCAN_ROWS, 1), False)):
        sp, ar = pair(arr, shape, tr)
        in_specs += sp
        args += ar
    out_specs = [pl.BlockSpec((SCAN_ROWS, SCAN_HEADS * HEAD_DIM), lambda h, t, d=d: (rb[d](t), h)) for d in range(2)]
    return pl.pallas_call(
        _dn_scan_kernel,
        grid=(N_HEADS // SCAN_HEADS, nlb + 1),
        in_specs=in_specs,
        out_specs=out_specs,
        out_shape=[jax.ShapeDtypeStruct((r, DN_DIM), F32)] * 2,
        scratch_shapes=[pltpu.VMEM((2, SCAN_HEADS, HEAD_DIM, HEAD_DIM), F32)],
        compiler_params=_cparams(("arbitrary", "arbitrary")),
        name="dn_scan",
    )(*args)


def _dn_out_kernel(of_ref, ob_ref, gate_ref, ng_ref, y_ref):
    for h in range(N_HEADS):
        cs = slice(h * HEAD_DIM, (h + 1) * HEAD_DIM)
        o = of_ref[:, cs] + ob_ref[:, cs]
        o = o * lax.rsqrt(jnp.mean(o * o, axis=-1, keepdims=True) + 1e-6) * ng_ref[...]
        y_ref[:, cs] = (o * _silu(gate_ref[:, cs])).astype(y_ref.dtype)


def _dn_out(o_f, o_b, gate, norm_g):
    r, n = o_f.shape
    row = pl.BlockSpec((ELT_ROWS, n), lambda i: (i, 0))
    return pl.pallas_call(
        _dn_out_kernel,
        grid=(r // ELT_ROWS,),
        in_specs=[row, row, row, pl.BlockSpec((1, HEAD_DIM), lambda i: (0, 0))],
        out_specs=row,
        out_shape=jax.ShapeDtypeStruct((r, n), BF16),
        compiler_params=_cparams(("arbitrary",)),
        name="dn_out",
    )(o_f, o_b, gate, norm_g.astype(F32).reshape(1, HEAD_DIM))


S5_TG = LANES // S5_GROUP
S5_FOLD = S5_CHUNK * LANES
S5_SW = S5_TG * 2 * S5_STATE
S5_PAIR = 2 * SUBLANES


def _s5_tile_params(lam_re, lam_im, log_step, b_re, b_im, c_re, c_im):
    f = lambda a: a.astype(F32)
    lr, li, br, bi, cr, ci = map(f, (lam_re, lam_im, b_re, b_im, c_re, c_im))
    dt = jnp.exp(f(log_step))[..., None]
    mag, ang = jnp.exp(lr * dt), li * dt
    lbr, lbi = mag * jnp.cos(ang), mag * jnp.sin(ang)
    den = lr * lr + li * li
    zr = ((lbr - 1.0) * lr + lbi * li) / den
    zi = (lbi * lr - (lbr - 1.0) * li) / den
    bbr = zr[..., None] * br - zi[..., None] * bi
    bbi = zr[..., None] * bi + zi[..., None] * br
    g, p = lbr.shape[1], lbr.shape[2]
    nt = g // S5_TG
    tile_rows = lambda a: a.reshape(2, nt, LANES, p)
    per_row = lambda a: tile_rows(jnp.repeat(a, S5_GROUP, axis=1))
    cat = lambda a, b: jnp.concatenate([a, b], axis=-1)
    lrr = cat(per_row(lbr), per_row(lbr))
    lii = cat(-per_row(lbi), per_row(lbi))
    bbc = cat(tile_rows(jnp.swapaxes(bbr, 2, 3)), tile_rows(jnp.swapaxes(bbi, 2, 3)))
    cc = cat(tile_rows(cr), tile_rows(ci))

    def cmul(xr, xi, yr, yi):
        return xr * yr - xi * yi, xr * yi + xi * yr

    ar, ai = lbr, lbi
    for _ in range(int(math.log2(S5_CHUNK))):
        ar, ai = cmul(ar, ai, ar, ai)
    pw = [(ar, ai)]
    for _ in range(SUBLANES - 1):
        pw.append(cmul(*pw[-1], ar, ai))
    rows = lambda xr, xi: jnp.stack([cat(xr, xr), cat(-xi, xi)], axis=2)
    steps = jnp.stack([jnp.broadcast_to(rows(*pw[n - 1])[:, :, :, None, :], (2, g, 2, SUBLANES, 2 * p)) for n in (1, 2, 4)], axis=2)
    fwd = jnp.stack([rows(*pw[j])[0] for j in range(SUBLANES)], axis=2)
    bwd = jnp.stack([rows(*pw[SUBLANES - 1 - j])[1] for j in range(SUBLANES)], axis=2)
    consts = jnp.concatenate([steps, jnp.stack([fwd, bwd])[:, :, None]], axis=2)
    consts = consts.reshape(2, nt, S5_TG, 4, 2, SUBLANES, 2 * p).transpose(1, 0, 3, 4, 5, 2, 6)
    return lrr, lii, bbc, cc, consts.reshape(nt, 2, 4, 2, SUBLANES, S5_SW)


def _cmul_lanes(x, rr, ii):
    return x * rr + pltpu.roll(x, S5_STATE, 1) * ii


def _s5_ops_kernel(lrr_ref, lii_ref, bbc_ref, cc_ref, *out_refs, toeplitz):
    row = lax.broadcasted_iota(jnp.int32, (LANES, LANES), 0)
    lane = lax.broadcasted_iota(jnp.int32, (LANES, LANES), 1)
    same_group = (row // S5_GROUP) == (lane // S5_GROUP)
    conj = jnp.where(lane < S5_STATE, 1.0, -1.0)
    c = S5_CHUNK
    lag = []
    for d in range(2):
        rr, ii = lrr_ref[d, 0], lii_ref[d, 0]
        xs, zs = [bbc_ref[d, 0]], [cc_ref[d, 0]]
        for _ in range(c):
            xs.append(_cmul_lanes(xs[-1], rr, ii))
            zs.append(_cmul_lanes(zs[-1], rr, ii))
        if toeplitz:
            zc = zs[0] * conj
            z1, z2 = _split2(zc)
            ks = []
            for l in range(c):
                x1, x2 = _split2(xs[l])
                k = (lax.dot_general(x1, z1, _NT, preferred_element_type=F32)
                     + lax.dot_general(x1, z2, _NT, preferred_element_type=F32)
                     + lax.dot_general(x2, z1, _NT, preferred_element_type=F32))
                ks.append(jnp.where(same_group, k, 0.0))
            lag.append(ks)
        else:
            bc_ref, cct_ref = out_refs
            for i in range(c):
                xin = xs[c - 1 - i] if d == 0 else xs[i]
                zout = (zs[i + 1] if d == 0 else zs[c - i]) * conj
                rs = slice(i * LANES, (i + 1) * LANES)
                for t in range(S5_TG):
                    sel = (row // S5_GROUP) == t
                    cs = slice(t * LANES, (t + 1) * LANES)
                    bc_ref[d, 0, rs, cs] = jnp.where(sel, xin, 0.0).astype(BF16)
                    cct_ref[d, 0, rs, cs] = jnp.where(sel, zout, 0.0).astype(BF16)
    if toeplitz:
        (m_ref,) = out_refs
        fwd = [k.astype(BF16) for k in lag[0]]
        bwd = [k.astype(BF16) for k in lag[1]]
        diag = (lag[0][0] + lag[1][0]).astype(BF16)
        for i in range(c):
            for j in range(c):
                blk = diag if i == j else (fwd[j - i] if j > i else bwd[i - j])
                m_ref[0, i * LANES:(i + 1) * LANES, j * LANES:(j + 1) * LANES] = blk


def _s5_ops(lrr, lii, bbc, cc):
    nt = lrr.shape[1]
    par = pl.BlockSpec((2, 1, LANES, LANES), lambda q: (0, q, 0, 0))
    common = dict(grid=(nt,), in_specs=[par] * 4, compiler_params=_cparams(("arbitrary",)))
    m = pl.pallas_call(
        functools.partial(_s5_ops_kernel, toeplitz=True),
        out_specs=pl.BlockSpec((1, S5_FOLD, S5_FOLD), lambda q: (q, 0, 0)),
        out_shape=jax.ShapeDtypeStruct((nt, S5_FOLD, S5_FOLD), BF16),
        name="s5_ops_toeplitz", **common)(lrr, lii, bbc, cc)
    st = pl.BlockSpec((2, 1, S5_FOLD, S5_SW), lambda q: (0, q, 0, 0))
    bc, cct = pl.pallas_call(
        functools.partial(_s5_ops_kernel, toeplitz=False),
        out_specs=[st, st],
        out_shape=[jax.ShapeDtypeStruct((2, nt, S5_FOLD, S5_SW), BF16)] * 2,
        name="s5_ops_state", **common)(lrr, lii, bbc, cc)
    return m, bc, cct


def _s5_state_kernel(uf_ref, bc_ref, k_ref, e_ref, vf_ref, vb_ref, *, n_lat_c, n_ctx_c):
    n_all = n_lat_c + n_ctx_c
    u = uf_ref[0]
    vf = _dot(u, bc_ref[0, 0])
    vf_ref[pl.ds(0, n_ctx_c), :] = vf[n_lat_c:]
    vf_ref[pl.ds(n_ctx_c, n_lat_c), :] = vf[:n_lat_c]
    vb_ref[...] = _dot(u, bc_ref[1, 0])
    sub = lax.broadcasted_iota(jnp.int32, (SUBLANES, LANES), 0)

    lanes = [(d, slice(g * LANES, (g + 1) * LANES)) for d in range(2) for g in range(S5_TG)]

    def tile_scans(xs, carry_rows):
        for n, sh in enumerate((1, 2, 4)):
            shifted = [jnp.where(sub >= sh, pltpu.roll(x, sh, 0), 0.0) if d == 0 else
                       jnp.where(sub < SUBLANES - sh, pltpu.roll(x, SUBLANES - sh, 0), 0.0)
                       for x, (d, _) in zip(xs, lanes)]
            xs = [x + _cmul_lanes(s, k_ref[0, d, n, 0, :, cs], k_ref[0, d, n, 1, :, cs])
                  for x, s, (d, cs) in zip(xs, shifted, lanes)]
        cbs = [jnp.broadcast_to(c, (SUBLANES, LANES)) for c in carry_rows]
        xs = [x + _cmul_lanes(cb, k_ref[0, d, 3, 0, :, cs], k_ref[0, d, 3, 1, :, cs])
              for x, cb, (d, cs) in zip(xs, cbs, lanes)]
        return [(jnp.where(sub == 0, cb, pltpu.roll(x, 1, 0)), x[SUBLANES - 1:SUBLANES, :]) if d == 0 else
                (jnp.where(sub == SUBLANES - 1, cb, pltpu.roll(x, SUBLANES - 1, 0)), x[0:1, :])
                for x, cb, (d, _) in zip(xs, cbs, lanes)]

    n_iter = n_all // S5_PAIR
    src = (vf_ref, vb_ref)

    def body(t, carries):
        base = (pl.multiple_of(t * S5_PAIR, S5_PAIR), pl.multiple_of((n_iter - 1 - t) * S5_PAIR, S5_PAIR))
        first = tile_scans([src[d][pl.ds(base[d] + d * SUBLANES, SUBLANES), cs] for d, cs in lanes], carries)
        second = tile_scans([src[d][pl.ds(base[d] + (1 - d) * SUBLANES, SUBLANES), cs] for d, cs in lanes],
                            [c for _, c in first])
        for (d, cs), (e1, _), (e2, _) in zip(lanes, first, second):
            lo, hi = (e1, e2) if d == 0 else (e2, e1)
            e_ref[d, 0, pl.ds(base[d], S5_PAIR), cs] = jnp.concatenate([lo, hi], axis=0).astype(BF16)
        return tuple(c for _, c in second)

    lax.fori_loop(0, n_iter, body, tuple(jnp.zeros((1, LANES), F32) for _ in lanes))


def _s5_out_kernel(uf_ref, m_ref, cct_ref, e_ref, y_ref, *, n_lat_c, n_ctx_c):
    y = _dot(uf_ref[0, pl.ds(0, n_lat_c), :], m_ref[0])
    y = y + lax.dot_general(e_ref[0, 0, pl.ds(n_ctx_c, n_lat_c), :], cct_ref[0, 0], _NT, preferred_element_type=F32)
    y = y + lax.dot_general(e_ref[1, 0, pl.ds(0, n_lat_c), :], cct_ref[1, 0], _NT, preferred_element_type=F32)
    y_ref[0] = y


def _s5_scan(uf, m, bc, cct, consts, *, n_lat):
    nt, n_all, _ = uf.shape
    n_lat_c = n_lat // S5_CHUNK
    n_ctx_c = n_all - n_lat_c
    assert n_lat_c % S5_PAIR == 0 and n_ctx_c % S5_PAIR == 0
    dims = dict(n_lat_c=n_lat_c, n_ctx_c=n_ctx_c)
    e = pl.pallas_call(
        functools.partial(_s5_state_kernel, **dims),
        grid=(nt,),
        in_specs=[
            pl.BlockSpec((1, n_all, S5_FOLD), lambda q: (q, 0, 0)),
            pl.BlockSpec((2, 1, S5_FOLD, S5_SW), lambda q: (0, q, 0, 0)),
            pl.BlockSpec((1, 2, 4, 2, SUBLANES, S5_SW), lambda q: (q, 0, 0, 0, 0, 0)),
        ],
        out_specs=pl.BlockSpec((2, 1, n_all, S5_SW), lambda q: (0, q, 0, 0)),
        out_shape=jax.ShapeDtypeStruct((2, nt, n_all, S5_SW), BF16),
        scratch_shapes=[pltpu.VMEM((n_all, S5_SW), F32), pltpu.VMEM((n_all, S5_SW), F32)],
        compiler_params=_cparams(("arbitrary",)),
        name="s5_state",
    )(uf, bc, consts)
    halves = S5_FOLD // S5_SW
    return pl.pallas_call(
        functools.partial(_s5_out_kernel, **dims),
        grid=(nt, halves),
        in_specs=[
            pl.BlockSpec((1, n_all, S5_FOLD), lambda q, j: (q, 0, 0)),
            pl.BlockSpec((1, S5_FOLD, S5_SW), lambda q, j: (q, 0, j)),
            pl.BlockSpec((2, 1, S5_SW, S5_SW), lambda q, j: (0, q, j, 0)),
            pl.BlockSpec((2, 1, n_all, S5_SW), lambda q, j: (0, q, 0, 0)),
        ],
        out_specs=pl.BlockSpec((1, n_lat_c, S5_SW), lambda q, j: (q, 0, j)),
        out_shape=jax.ShapeDtypeStruct((nt, n_lat_c, S5_FOLD), F32),
        compiler_params=_cparams(("arbitrary", "arbitrary")),
        name="s5_out",
    )(uf, m, cct, e)


def _gelu_skip_kernel(y_ref, u_ref, d_ref, o_ref):
    z = y_ref[...] + d_ref[...] * u_ref[...]
    inner = math.sqrt(2.0 / math.pi) * (z + 0.044715 * (z * z * z))
    o_ref[...] = (0.5 * z * (1.0 + jnp.tanh(inner))).astype(o_ref.dtype)


def _gelu_skip(y, u, d_skip):
    r, d = y.shape
    row = pl.BlockSpec((ELT_ROWS, d), lambda i: (i, 0))
    return pl.pallas_call(
        _gelu_skip_kernel,
        grid=(r // ELT_ROWS,),
        in_specs=[row, row, pl.BlockSpec((1, d), lambda i: (0, 0))],
        out_specs=row,
        out_shape=jax.ShapeDtypeStruct((r, d), BF16),
        compiler_params=_cparams(("arbitrary",)),
        name="gelu_skip",
    )(y, u, d_skip.astype(F32).reshape(1, d))


def _mixer_na_gdn(u, w_in, w_out, rpb, conv_w, a_log, dt_bias, norm_g, *, n_lat):
    r = u.shape[0]
    n_main = 3 * NA_DIM + 4 * DN_DIM
    qkv_na = _matmul(u, w_in, col0=0, ncols=3 * NA_DIM, tn=256, out_dtype=BF16)
    qkv_dn = _matmul(u, w_in, col0=3 * NA_DIM, ncols=3 * DN_DIM, tn=256, out_dtype=F32)
    gate = _matmul(u, w_in, col0=3 * NA_DIM + 3 * DN_DIM, ncols=DN_DIM, tn=256, out_dtype=F32)
    w_ab = jnp.pad(w_in[:, n_main:], ((0, 0), (0, LANES - 4 * N_HEADS)))
    ab = _matmul(u, w_ab, col0=0, ncols=LANES, tn=LANES, out_dtype=F32)

    o_na = _na_attention(qkv_na, _na_bias_table(rpb), n_lat=n_lat)

    cos, sin = _rope_tables(n_lat)
    qk = _dn_conv(qkv_dn, conv_w.astype(F32), cos, sin, n_lat=n_lat, col0=0, ncols=2 * DN_DIM, qk=True)
    v = _dn_conv(qkv_dn, conv_w.astype(F32), cos, sin, n_lat=n_lat, col0=2 * DN_DIM, ncols=DN_DIM, qk=False)
    gb = _dn_gates(ab, a_log, dt_bias)
    gb_t = gb.T.reshape(4, 2, N_HEADS, r)
    beta_col = gb_t[1].reshape(2, N_HEADS, r, 1)
    gc_col = gb_t[2].reshape(2, N_HEADS, r, 1)
    gc_row = gb_t[2].reshape(2, N_HEADS, 1, r)
    tot_col = gb_t[3].reshape(2, N_HEADS, r, 1)
    u_c, w_c, qd, kdt, qkm = _dn_prep(qk, v, gc_col, gc_row, tot_col, beta_col)
    o_f, o_b = _dn_scan(u_c, w_c, qd, kdt, qkm, tot_col, n_lat=n_lat)
    o_dn = _dn_out(o_f, o_b, gate, norm_g)
    return _matmul_cat(o_na, o_dn, w_out, tn=512, out_dtype=BF16)


def _mixer_s5(u_bf, u_f32, lam_re, lam_im, log_step, b_re, b_im, c_re, c_im, d_skip, w_out, w_gate, *, n_lat):
    r, d = u_bf.shape
    nt = d // LANES
    uf = u_bf.reshape(r, nt, LANES).transpose(1, 0, 2).reshape(nt, r // S5_CHUNK, S5_FOLD)
    lrr, lii, bbc, cc, consts = _s5_tile_params(lam_re, lam_im, log_step, b_re, b_im, c_re, c_im)
    m, bc, cct = _s5_ops(lrr, lii, bbc, cc)
    yf = _s5_scan(uf, m, bc, cct, consts, n_lat=n_lat)
    y = yf.reshape(nt, n_lat, LANES).transpose(1, 0, 2).reshape(n_lat, d)
    gz = _gelu_skip(y, u_f32, d_skip)
    return _matmul_pair(gz, w_out, w_gate, col1=0, col2=0, ncols=d, tn=256, gate_second=True, out_dtype=BF16)


def kernel(x, c, ctx, c_ctx, w_mod, b_mod, ln_g, ln_b, ffn_w_in, ffn_w_out, ab_w_in, ab_w_out, na_rpb, dn_conv_w, dn_a_log, dn_dt_bias, dn_norm_g, s5_lam_re, s5_lam_im, s5_log_step, s5_b_re, s5_b_im, s5_c_re, s5_c_im, s5_d, s5_w_out, s5_w_gate):
    depth, d = w_mod.shape[0], x.shape[2]
    assert depth == 2 and x.shape[0] == 1, "layer 0 is the NA/DeltaNet layer, layer 1 the final S5 layer"
    n_lat = x.shape[1]
    alpha = (2.0 * depth) ** 0.25
    h = jnp.concatenate([x[0], ctx[0]], axis=0)
    mods = _modulation(c, c_ctx, w_mod, b_mod)
    lng = ln_g.astype(F32).reshape(depth * 3, 1, d)
    lnb = ln_b.astype(F32).reshape(depth * 3, 1, d)
    res_ln = functools.partial(_res_ln, mods=mods, ln_g=lng, ln_b=lnb, alpha=alpha, n_lat=n_lat)
    w_out_bf = ffn_w_out.astype(BF16)
    ffn = lambda uu, l, s: _ffn(uu, ffn_w_in, w_out_bf, (l, s))

    u = _modulate(h, mods, layer=0, sub=0, n_lat=n_lat)
    h, u = res_ln(h, ffn(u, 0, 0), layer=0, sub=0, weight=MACARON_WEIGHT, next_mod=(0, 1))
    y = _mixer_na_gdn(u, ab_w_in[0], ab_w_out[0], na_rpb[0], dn_conv_w[0], dn_a_log[0], dn_dt_bias[0], dn_norm_g[0],
                      n_lat=n_lat)
    h, u = res_ln(h, y, layer=0, sub=1, weight=1.0, next_mod=(0, 2))
    h, u = res_ln(h, ffn(u, 0, 1), layer=0, sub=2, weight=MACARON_WEIGHT, next_mod=(1, 0))

    h, u, u32 = res_ln(h, ffn(u, 1, 0), layer=1, sub=0, weight=MACARON_WEIGHT, next_mod=(1, 1), u_dtypes=(BF16, F32))
    y = _mixer_s5(u, u32, s5_lam_re[0], s5_lam_im[0], s5_log_step[0], s5_b_re[0], s5_b_im[0], s5_c_re[0], s5_c_im[0],
                  s5_d[0], s5_w_out[0], s5_w_gate[0], n_lat=n_lat)
    h, u = res_ln(h, y, layer=1, sub=1, weight=1.0, next_mod=(1, 2))
    (out,) = res_ln(h, ffn(u, 1, 1), layer=1, sub=2, weight=MACARON_WEIGHT, next_mod=None)
    return out[None]
```

```python
import functools
import math

import numpy as np
import jax
import jax.numpy as jnp
from jax import lax
from jax.experimental import pallas as pl
from jax.experimental.pallas import tpu as pltpu

GRID_W = 64
HEAD_DIM = 128
N_HEADS = 16
NA_DIM = N_HEADS * HEAD_DIM
DN_DIM = N_HEADS * HEAD_DIM
NA_WIN_ROWS = 8
NA_WIN_COLS = 16
DN_CONV = 5
DN_CHUNK = 64
ROPE_BASE = 10000.0
S5_GROUP = 16
S5_STATE = 64
S5_CHUNK = 16
N_MOD = 9
LN_EPS = 1e-5
MACARON_WEIGHT = 0.5

V7X_VMEM_BYTES = 64 * 1024 * 1024
VMEM_LIMIT = V7X_VMEM_BYTES - 8 * 1024 * 1024
LANES = 128
SUBLANES = 8

F32 = jnp.float32
BF16 = jnp.bfloat16


def _cparams(sem):
    return pltpu.CompilerParams(dimension_semantics=sem, vmem_limit_bytes=VMEM_LIMIT)


def _silu(x):
    return x * (1.0 / (1.0 + jnp.exp(-x)))


def _sigmoid(x):
    return 1.0 / (1.0 + jnp.exp(-x))


def _row_tile(rows, candidates=(768, 512, 256)):
    for t in candidates:
        if rows % t == 0:
            return t
    raise ValueError(f"unsupported row count {rows}")


F32_WEIGHT_ROW_TILES = (1408, 1024, 768, 512, 256)


def _weight_row_tile(rows, w):
    return _row_tile(rows, F32_WEIGHT_ROW_TILES if w.dtype == F32 else (768, 512, 256))


def _weight_spec(lead, k, tn, col_block0, n_grid):
    if n_grid == 2:
        imap = lambda i, j: lead + (0, j + col_block0)
    else:
        imap = lambda i, j, kk: lead + (kk, j + col_block0)
    return pl.BlockSpec((None,) * len(lead) + (k, tn), imap)


def _mod_kernel(c_ref, w_ref, b_ref, o_ref):
    s = _silu(c_ref[...]).astype(BF16)
    w = w_ref[0].astype(BF16)
    o_ref[0] = jnp.dot(s, w, preferred_element_type=F32) + b_ref[0]


def _modulation(c, c_ctx, w_mod, b_mod):
    depth, d, n = w_mod.shape
    tn = 512
    cc = jnp.zeros((SUBLANES, d), F32).at[0].set(c[0]).at[1].set(c_ctx)
    out = pl.pallas_call(
        _mod_kernel,
        grid=(depth, n // tn),
        in_specs=[
            pl.BlockSpec((SUBLANES, d), lambda l, j: (0, 0)),
            pl.BlockSpec((1, d, tn), lambda l, j: (l, 0, j)),
            pl.BlockSpec((1, 1, tn), lambda l, j: (l, 0, j)),
        ],
        out_specs=pl.BlockSpec((1, SUBLANES, tn), lambda l, j: (l, 0, j)),
        out_shape=jax.ShapeDtypeStruct((depth, SUBLANES, n), F32),
        compiler_params=_cparams(("arbitrary", "arbitrary")),
        name="modulation",
    )(cc, w_mod, b_mod.reshape(depth, 1, n))
    return out[:, :2].reshape(depth * 2 * N_MOD, 1, d)


def _mm_kernel(a_ref, w_ref, o_ref):
    o_ref[...] = jnp.dot(a_ref[...], w_ref[...].astype(BF16), preferred_element_type=F32).astype(o_ref.dtype)


def _matmul(a, w, *, lead=(), col0, ncols, tn, out_dtype):
    r, k = a.shape
    tm = _weight_row_tile(r, w)
    assert col0 % tn == 0 and ncols % tn == 0 and w.shape[len(lead)] == k
    return pl.pallas_call(
        _mm_kernel,
        grid=(r // tm, ncols // tn),
        in_specs=[pl.BlockSpec((tm, k), lambda i, j: (i, 0)), _weight_spec(lead, k, tn, col0 // tn, 2)],
        out_specs=pl.BlockSpec((tm, tn), lambda i, j: (i, j)),
        out_shape=jax.ShapeDtypeStruct((r, ncols), out_dtype),
        compiler_params=_cparams(("arbitrary", "arbitrary")),
        name="matmul",
    )(a, w)


def _mm_pair_kernel(a_ref, w1_ref, w2_ref, o_ref, *, gate_second):
    a = a_ref[...]
    p1 = jnp.dot(a, w1_ref[...].astype(BF16), preferred_element_type=F32)
    p2 = jnp.dot(a, w2_ref[...].astype(BF16), preferred_element_type=F32)
    if gate_second:
        o = p1 * _sigmoid(p2)
    else:
        o = _silu(p1) * p2
    o_ref[...] = o.astype(o_ref.dtype)


def _matmul_pair(a, w1, w2, *, lead=(), col1, col2, ncols, tn, gate_second, out_dtype):
    r, k = a.shape
    tm = _weight_row_tile(r, w1)
    assert col1 % tn == 0 and col2 % tn == 0 and ncols % tn == 0
    return pl.pallas_call(
        functools.partial(_mm_pair_kernel, gate_second=gate_second),
        grid=(r // tm, ncols // tn),
        in_specs=[pl.BlockSpec((tm, k), lambda i, j: (i, 0)),
                  _weight_spec(lead, k, tn, col1 // tn, 2), _weight_spec(lead, k, tn, col2 // tn, 2)],
        out_specs=pl.BlockSpec((tm, tn), lambda i, j: (i, j)),
        out_shape=jax.ShapeDtypeStruct((r, ncols), out_dtype),
        compiler_params=_cparams(("arbitrary", "arbitrary")),
        name="matmul_pair",
    )(a, w1, w2)


def _mm_cat_kernel(a1_ref, a2_ref, w_ref, o_ref, acc_ref):
    k = pl.program_id(2)

    @pl.when(k == 0)
    def _():
        acc_ref[...] = jnp.dot(a1_ref[...], w_ref[...].astype(BF16), preferred_element_type=F32)

    @pl.when(k == 1)
    def _():
        o_ref[...] = (acc_ref[...]
                      + jnp.dot(a2_ref[...], w_ref[...].astype(BF16), preferred_element_type=F32)).astype(o_ref.dtype)


def _matmul_cat(a1, a2, w, *, tn, out_dtype):
    r, k1 = a1.shape
    assert a2.shape == (r, k1) and w.shape[0] == 2 * k1
    n = w.shape[1]
    tm = _weight_row_tile(r, w)
    return pl.pallas_call(
        _mm_cat_kernel,
        grid=(r // tm, n // tn, 2),
        in_specs=[
            pl.BlockSpec((tm, k1), lambda i, j, k: (i, 0)),
            pl.BlockSpec((tm, k1), lambda i, j, k: (i, 0)),
            _weight_spec((), k1, tn, 0, 3),
        ],
        out_specs=pl.BlockSpec((tm, tn), lambda i, j, k: (i, j)),
        out_shape=jax.ShapeDtypeStruct((r, n), out_dtype),
        scratch_shapes=[pltpu.VMEM((tm, tn), F32)],
        compiler_params=_cparams(("arbitrary", "arbitrary", "arbitrary")),
        name="matmul_cat",
    )(a1, a2, w)


ELT_ROWS = 256


def _modulate_kernel(z_ref, shift_ref, scale_ref, u_ref):
    u_ref[...] = (z_ref[...] * (1.0 + scale_ref[0]) + shift_ref[0]).astype(u_ref.dtype)


def _mod_row(layer, r, n_lat_tiles):
    return lambda i: ((layer * 2 + jnp.where(i >= n_lat_tiles, 1, 0)) * N_MOD + r, 0, 0)


def _modulate(z, mods, *, layer, sub, n_lat):
    r, d = z.shape
    nlt = n_lat // ELT_ROWS
    vec = lambda row: pl.BlockSpec((1, 1, d), _mod_row(layer, row, nlt))
    return pl.pallas_call(
        _modulate_kernel,
        grid=(r // ELT_ROWS,),
        in_specs=[pl.BlockSpec((ELT_ROWS, d), lambda i: (i, 0)), vec(3 * sub), vec(3 * sub + 1)],
        out_specs=pl.BlockSpec((ELT_ROWS, d), lambda i: (i, 0)),
        out_shape=jax.ShapeDtypeStruct((r, d), BF16),
        compiler_params=_cparams(("arbitrary",)),
        name="modulate",
    )(z, mods, mods)


def _res_ln_kernel(*refs, alpha, weight, n_u):
    z_ref, y_ref, gate_ref, g_ref, b_ref = refs[:5]
    t = alpha * z_ref[...] + (weight * gate_ref[0]) * y_ref[...].astype(F32)
    mu = jnp.mean(t, axis=-1, keepdims=True)
    tc = t - mu
    var = jnp.mean(tc * tc, axis=-1, keepdims=True)
    x = tc * lax.rsqrt(var + LN_EPS) * g_ref[0] + b_ref[0]
    if n_u == 0:
        refs[5][...] = x
        return
    shift_ref, scale_ref, x_ref = refs[5], refs[6], refs[7]
    x_ref[...] = x
    u = x * (1.0 + scale_ref[0]) + shift_ref[0]
    for u_ref in refs[8:]:
        if len(u_ref.shape) == 3:
            for q in range(u_ref.shape[0]):
                u_ref[q] = u[:, q * LANES:(q + 1) * LANES].astype(u_ref.dtype)
        else:
            u_ref[...] = u.astype(u_ref.dtype)


def _res_ln(z, y, *, mods, ln_g, ln_b, layer, sub, weight, alpha, n_lat, next_mod, u_dtypes=(BF16,), u_tiles=False):
    r, d = y.shape
    nlt = n_lat // ELT_ROWS
    vec = lambda lyr, row: pl.BlockSpec((1, 1, d), _mod_row(lyr, row, nlt))
    ln_idx = layer * 3 + sub
    ln_spec = pl.BlockSpec((1, 1, d), lambda i: (ln_idx, 0, 0))
    row_spec = pl.BlockSpec((ELT_ROWS, d), lambda i: (i, 0))
    in_specs = [row_spec, row_spec, vec(layer, 3 * sub + 2), ln_spec, ln_spec]
    args = [z, y, mods, ln_g, ln_b]
    out_specs = [row_spec]
    out_shape = [jax.ShapeDtypeStruct((r, d), F32)]
    n_u = 0
    if next_mod is not None:
        nl, ns = next_mod
        in_specs += [vec(nl, 3 * ns), vec(nl, 3 * ns + 1)]
        args += [mods, mods]
        n_u = len(u_dtypes)
        if u_tiles:
            out_specs += [pl.BlockSpec((d // LANES, ELT_ROWS, LANES), lambda i: (0, i, 0))] * n_u
            out_shape += [jax.ShapeDtypeStruct((d // LANES, r, LANES), dt) for dt in u_dtypes]
        else:
            out_specs += [row_spec] * n_u
            out_shape += [jax.ShapeDtypeStruct((r, d), dt) for dt in u_dtypes]
    res = pl.pallas_call(
        functools.partial(_res_ln_kernel, alpha=alpha, weight=weight, n_u=n_u),
        grid=(r // ELT_ROWS,),
        in_specs=in_specs,
        out_specs=out_specs,
        out_shape=out_shape,
        compiler_params=_cparams(("arbitrary",)),
        name="res_ln",
    )(*args)
    return res


def _ffn(u, w_in, w_out, lead):
    d_ff, d = w_out.shape[-2:]
    hmid = _matmul_pair(u, w_in, w_in, lead=lead, col1=0, col2=d_ff, ncols=d_ff, tn=256, gate_second=False,
                        out_dtype=BF16)
    return _matmul(hmid, w_out, lead=lead, col0=0, ncols=d, tn=512, out_dtype=BF16)


NEG_BIG = -1e30
NA_UNROLL = 8
_NT =(((1,), (1,)), ((), ()))


def _na_bias_table(rpb):
    kr, kw, w = NA_WIN_ROWS, NA_WIN_COLS, GRID_W
    pat = np.arange(kr)[:, None]
    i = np.arange(kr)[None, :]
    ridx = i - pat + (NA_WIN_ROWS - 1)
    col = np.arange(w)
    cs = np.clip(col - kw // 2, 0, w - kw)
    c2 = np.arange(w)[None, :]
    valid = (c2 >= cs[:, None]) & (c2 < cs[:, None] + kw)
    cidx = np.clip(c2 - col[:, None] + (NA_WIN_COLS - 1), 0, 2 * NA_WIN_COLS - 2)
    row_sel = (ridx[:, :, None] == np.arange(2 * kr - 1)).astype(np.float32)
    col_sel = ((cidx[:, :, None] == np.arange(2 * kw - 1)) & valid[:, :, None]).astype(np.float32)
    by_row = jnp.einsum('pia,hab->hpib', jnp.asarray(row_sel), rpb.astype(F32), precision=lax.Precision.HIGHEST)
    tbl = jnp.einsum('hpib,cdb->hpcid', by_row, jnp.asarray(col_sel), precision=lax.Precision.HIGHEST)
    tbl = tbl + jnp.asarray(np.where(valid, 0.0, NEG_BIG).astype(np.float32))[None, None, :, None, :]
    return tbl.reshape(rpb.shape[0], kr, w, kr * w)


def _na_kernel(q_ref, k_ref, v_ref, bias_ref, o_ref, *, n_lat, n_ctx):
    rows = n_lat // GRID_W
    kr = NA_WIN_ROWS
    scale = HEAD_DIM ** -0.5
    kc = k_ref[pl.ds(n_lat, n_ctx), :]
    vc = v_ref[pl.ds(n_lat, n_ctx), :]

    def attend(s_parts, v_parts):
        m = functools.reduce(jnp.maximum, [jnp.max(s, axis=-1, keepdims=True) for s in s_parts])
        ps = [jnp.exp(s - m) for s in s_parts]
        den = functools.reduce(lambda a, b: a + b, [jnp.sum(p, axis=-1, keepdims=True) for p in ps])
        num = functools.reduce(lambda a, b: a + b,
                               [jnp.dot(p.astype(BF16), v, preferred_element_type=F32) for p, v in zip(ps, v_parts)])
        return num / den

    def body(it, carry):
        rs = [it * NA_UNROLL + n for n in range(NA_UNROLL)]
        r0s = [jnp.clip(r - kr // 2, 0, rows - kr) for r in rs]
        qrows = [pl.ds(pl.multiple_of(r * GRID_W, GRID_W), GRID_W) for r in rs]
        krows = [pl.ds(pl.multiple_of(r0 * GRID_W, GRID_W), kr * GRID_W) for r0 in r0s]
        qs = [q_ref[qr, :] for qr in qrows]
        s_loc = [lax.dot_general(q, k_ref[kr_, :], _NT, preferred_element_type=F32) * scale + bias_ref[0, r - r0]
                 for q, kr_, r, r0 in zip(qs, krows, rs, r0s)]
        s_ctx = [lax.dot_general(q, kc, _NT, preferred_element_type=F32) * scale for q in qs]
        outs = [attend([sl, sc], [v_ref[kr_, :], vc]) for sl, sc, kr_ in zip(s_loc, s_ctx, krows)]
        for qr, o in zip(qrows, outs):
            o_ref[qr, :] = o.astype(o_ref.dtype)
        return carry

    assert rows % NA_UNROLL == 0
    lax.fori_loop(0, rows // NA_UNROLL, body, 0)
    qc = q_ref[pl.ds(n_lat, n_ctx), :]
    s_cc = lax.dot_general(qc, kc, _NT, preferred_element_type=F32) * scale
    o_ref[pl.ds(n_lat, n_ctx), :] = attend([s_cc], [vc]).astype(o_ref.dtype)


def _na_attention(qkv, bias_tbl, *, n_lat):
    r = qkv.shape[0]
    n_ctx = r - n_lat
    assert n_lat // GRID_W >= NA_WIN_ROWS
    blk = lambda off: pl.BlockSpec((r, HEAD_DIM), lambda h: (0, h + off))
    return pl.pallas_call(
        functools.partial(_na_kernel, n_lat=n_lat, n_ctx=n_ctx),
        grid=(N_HEADS,),
        in_specs=[blk(0), blk(N_HEADS), blk(2 * N_HEADS),
                  pl.BlockSpec((1,) + bias_tbl.shape[1:], lambda h: (h, 0, 0, 0))],
        out_specs=pl.BlockSpec((r, HEAD_DIM), lambda h: (0, h)),
        out_shape=jax.ShapeDtypeStruct((r, NA_DIM), BF16),
        compiler_params=_cparams(("arbitrary",)),
        name="na_attention",
    )(qkv, qkv, qkv, bias_tbl)


CONV_ROWS = 256
PAD = SUBLANES


def _rope_tables(n_lat):
    t = np.arange(n_lat)
    n_freq = HEAD_DIM // 4
    inv = ROPE_BASE ** (-np.arange(n_freq, dtype=np.float32) / n_freq)
    ang = np.concatenate([(t // GRID_W).astype(np.float32)[:, None] * inv,
                          (t % GRID_W).astype(np.float32)[:, None] * inv], -1).astype(np.float32)
    ang = jnp.asarray(ang)
    cos = jnp.repeat(jnp.cos(ang), 2, axis=-1)
    sin = jnp.repeat(jnp.sin(ang), 2, axis=-1) * jnp.asarray(np.tile(np.array([-1.0, 1.0], np.float32), HEAD_DIM // 2))
    return cos, sin


def _dn_conv_kernel(x_ref, w_ref, cos_ref, sin_ref, o_ref, xp_ref, *, n_lat, n_ctx, qk):
    half = DN_CONV // 2
    zeros = jnp.zeros((PAD, LANES), F32)
    xp_ref[pl.ds(0, PAD), :] = zeros
    xp_ref[pl.ds(PAD, n_lat), :] = x_ref[pl.ds(0, n_lat), :]
    xp_ref[pl.ds(PAD + n_lat, PAD), :] = zeros
    xp_ref[pl.ds(2 * PAD + n_lat, n_ctx), :] = x_ref[pl.ds(n_lat, n_ctx), :]
    xp_ref[pl.ds(2 * PAD + n_lat + n_ctx, PAD), :] = zeros
    w = w_ref[...]
    lane = lax.broadcasted_iota(jnp.int32, (CONV_ROWS, LANES), 1)
    even = (lane % 2) == 0

    def chunk(src0, dst0, rope_row0):
        big = xp_ref[pl.ds(src0 - PAD, CONV_ROWS + 2 * PAD), :]
        acc = jnp.zeros((CONV_ROWS, LANES), F32)
        for j in range(DN_CONV):
            acc = acc + big[PAD + j - half: PAD + j - half + CONV_ROWS, :] * w[j:j + 1, :]
        y = _silu(acc)
        if qk:
            y = y * lax.rsqrt(jnp.sum(y * y, axis=-1, keepdims=True) + 1e-6)
            if rope_row0 is not None:
                swapped = jnp.where(even, pltpu.roll(y, LANES - 1, 1), pltpu.roll(y, 1, 1))
                y = y * cos_ref[pl.ds(rope_row0, CONV_ROWS), :] + swapped * sin_ref[pl.ds(rope_row0, CONV_ROWS), :]
        o_ref[pl.ds(dst0, CONV_ROWS), :] = y

    def lat_body(i, carry):
        r0 = pl.multiple_of(i * CONV_ROWS, CONV_ROWS)
        chunk(r0 + PAD, r0, r0)
        return carry

    lax.fori_loop(0, n_lat // CONV_ROWS, lat_body, 0)
    for i in range(n_ctx // CONV_ROWS):
        chunk(2 * PAD + n_lat + i * CONV_ROWS, n_lat + i * CONV_ROWS, None)


def _dn_conv(qkv, conv_w, cos, sin, *, n_lat, col0, ncols, qk):
    r = qkv.shape[0]
    n_ctx = r - n_lat
    assert n_lat % CONV_ROWS == 0 and n_ctx % CONV_ROWS == 0
    jb = col0 // LANES
    return pl.pallas_call(
        functools.partial(_dn_conv_kernel, n_lat=n_lat, n_ctx=n_ctx, qk=qk),
        grid=(ncols // LANES,),
        in_specs=[
            pl.BlockSpec((r, LANES), lambda j: (0, j + jb)),
            pl.BlockSpec((DN_CONV, LANES), lambda j: (0, j + jb)),
            pl.BlockSpec((n_lat, LANES), lambda j: (0, 0)),
            pl.BlockSpec((n_lat, LANES), lambda j: (0, 0)),
        ],
        out_specs=pl.BlockSpec((r, LANES), lambda j: (0, j)),
        out_shape=jax.ShapeDtypeStruct((r, ncols), F32),
        scratch_shapes=[pltpu.VMEM((r + 3 * PAD, LANES), F32)],
        compiler_params=_cparams(("arbitrary",)),
        name="dn_conv",
    )(qkv, conv_w, cos, sin)


def _dn_gates_kernel(ab_ref, alog_ref, dtb_ref, o_ref):
    ab = ab_ref[...]
    z = ab + dtb_ref[...]
    softplus = jnp.maximum(z, 0.0) + jnp.log(1.0 + jnp.exp(-jnp.abs(z)))
    lane = lax.broadcasted_iota(jnp.int32, ab.shape, 1)
    g = jnp.where(lane < 2 * N_HEADS, -jnp.exp(alog_ref[...]) * softplus, 0.0)
    n = ab.shape[0]
    i = lax.broadcasted_iota(jnp.int32, (n, n), 0)
    j = lax.broadcasted_iota(jnp.int32, (n, n), 1)
    same = (i // DN_CHUNK) == (j // DN_CHUNK)
    one = lambda m: jnp.where(m, 1.0, 0.0).astype(BF16)
    gc = jnp.where(lane < N_HEADS, _dot_exact_left(one(same & (i >= j)), g), _dot_exact_left(one(same & (i <= j)), g))
    tot = _dot_exact_left(one(same), g)
    quarter = LANES // 4
    o_ref[...] = jnp.where(lane < quarter, g,
                           jnp.where(lane < 2 * quarter, _sigmoid(ab),
                                     jnp.where(lane < 3 * quarter, pltpu.roll(gc, 2 * quarter, 1),
                                               pltpu.roll(tot, 3 * quarter, 1))))


def _dn_gates(ab, a_log, dt_bias):
    r = ab.shape[0]
    pad = lambda v: jnp.zeros((1, LANES), F32).at[0, :2 * N_HEADS].set(v.astype(F32).reshape(-1))
    return pl.pallas_call(
        _dn_gates_kernel,
        grid=(r // ELT_ROWS,),
        in_specs=[pl.BlockSpec((ELT_ROWS, LANES), lambda i: (i, 0)),
                  pl.BlockSpec((1, LANES), lambda i: (0, 0)),
                  pl.BlockSpec((1, LANES), lambda i: (0, 0))],
        out_specs=pl.BlockSpec((ELT_ROWS, LANES), lambda i: (i, 0)),
        out_shape=jax.ShapeDtypeStruct((r, LANES), F32),
        compiler_params=_cparams(("arbitrary",)),
        name="dn_gates",
    )(ab, pad(a_log), pad(dt_bias))


BLK = 2 * DN_CHUNK


def _split3(x):
    x1 = x.astype(BF16)
    r1 = x - x1.astype(F32)
    x2 = r1.astype(BF16)
    x3 = (r1 - x2.astype(F32)).astype(BF16)
    return x1, x2, x3


def _dot(a, b):
    return jnp.dot(a, b, preferred_element_type=F32)


def _dot_exact_left(m01, x):
    x1, x2, x3 = _split3(x)
    return _dot(m01, x1) + _dot(m01, x2) + _dot(m01, x3)


def _split2(x):
    x1 = x.astype(BF16)
    return x1, (x - x1.astype(F32)).astype(BF16)


def _dn_chunk_math(units):
    n = BLK
    i = lax.broadcasted_iota(jnp.int32, (n, n), 0)
    j = lax.broadcasted_iota(jnp.int32, (n, n), 1)
    same = (i // DN_CHUNK) == (j // DN_CHUNK)
    masks = {False: (same & (i >= j), same & (i > j)), True: (same & (i <= j), same & (i < j))}
    eye = jnp.where(i == j, 1.0, 0.0)
    each = lambda fn, *lists: [fn(*xs) for xs in zip(*lists)]
    q, k, v, gc_col, gc_row, tot_col, beta_col, rev = map(list, zip(*units))
    incl = [masks[r][0] for r in rev]
    strict = [masks[r][1] for r in rev]
    decay = each(lambda m, c, r: jnp.where(m, jnp.exp(jnp.where(m, c - r, 0.0)), 0.0), incl, gc_col, gc_row)
    kb = each(lambda a, b: a * b, k, beta_col)
    qs = [x * (HEAD_DIM ** -0.5) for x in q]
    kk_qk = each(lambda a, b, c: lax.dot_general(jnp.concatenate([a, b], axis=0).astype(BF16), c.astype(BF16), _NT,
                                                 preferred_element_type=F32), kb, qs, k)
    a = each(lambda m, x, dc: jnp.where(m, x[:n] * dc, 0.0), strict, kk_qk, decay)
    qk = each(lambda m, x, dc: jnp.where(m, x[n:] * dc, 0.0), incl, kk_qk, decay)
    t = [eye - x for x in a]
    ab = [x.astype(BF16) for x in a]
    p = [_dot(x, x) for x in ab]
    levels = int(math.log2(DN_CHUNK)) - 1
    for lvl in range(levels):
        last = lvl == levels - 1
        prod = [_dot((tt if last else jnp.concatenate([tt, pp], axis=0)).astype(BF16), pp.astype(BF16))
                for tt, pp in zip(t, p)]
        t = [tt + pr[:n] for tt, pr in zip(t, prod)]
        if not last:
            p = [pr[n:] for pr in prod]
    eg = [jnp.exp(c) for c in gc_col]
    rhs = each(lambda vv, b, kk, e: jnp.concatenate([(vv * b).astype(BF16), (kk * e).astype(BF16)], axis=1),
               v, beta_col, kb, eg)
    uw = [_dot(tt.astype(BF16), r) for tt, r in zip(t, rhs)]
    kdt = each(lambda kk, tc, c: (kk * jnp.exp(tc - c)).T.astype(BF16), k, tot_col, gc_col)
    return [(x[:, :HEAD_DIM], x[:, HEAD_DIM:].astype(BF16), (qq * e).astype(BF16), kt, m.astype(BF16))
            for x, qq, e, kt, m in zip(uw, qs, eg, kdt, qk)]


PREP_BLOCKS = 2
PREP_HEADS = 2


def _dn_prep_kernel(q_ref, k_ref, v_ref, gc_ref, gr_ref, tot_ref, beta_ref,
                    u_ref, w_ref, qd_ref, kdt_ref, qk_ref):
    where = [(d, hh, pl.ds(b * BLK, BLK)) for b in range(PREP_BLOCKS) for hh in range(PREP_HEADS) for d in range(2)]
    hcols = lambda hh: slice(hh * HEAD_DIM, (hh + 1) * HEAD_DIM)
    units = [(q_ref[rs, hcols(hh)], k_ref[rs, hcols(hh)], v_ref[rs, hcols(hh)], gc_ref[d, hh, rs, :],
              gr_ref[d, hh, :, rs], tot_ref[d, hh, rs, :], beta_ref[d, hh, rs, :], d == 1) for d, hh, rs in where]
    for (d, hh, rs), (u, w, qd, kdt, qk) in zip(where, _dn_chunk_math(units)):
        u_ref[d, hh, rs, :] = u
        w_ref[d, hh, rs, :] = w
        qd_ref[d, hh, rs, :] = qd
        kdt_ref[d, hh, :, rs] = kdt
        qk_ref[d, hh, rs, :] = qk


def _dn_prep(qk, v, gc_col, gc_row, tot_col, beta_col):
    r = v.shape[0]
    rb = PREP_BLOCKS * BLK
    assert r % rb == 0
    ph = PREP_HEADS
    head_blk = pl.BlockSpec((rb, ph * HEAD_DIM), lambda h, t: (t, h))
    k_blk = pl.BlockSpec((rb, ph * HEAD_DIM), lambda h, t: (t, h + N_HEADS // ph))
    col_blk = pl.BlockSpec((2, ph, rb, 1), lambda h, t: (0, h, t, 0))
    row_blk = pl.BlockSpec((2, ph, 1, rb), lambda h, t: (0, h, 0, t))
    out_blk = pl.BlockSpec((2, ph, rb, HEAD_DIM), lambda h, t: (0, h, t, 0))
    out_t_blk = pl.BlockSpec((2, ph, HEAD_DIM, rb), lambda h, t: (0, h, 0, t))
    sds = lambda shape, dt: jax.ShapeDtypeStruct((2, N_HEADS) + shape, dt)
    return pl.pallas_call(
        _dn_prep_kernel,
        grid=(N_HEADS // ph, r // rb),
        in_specs=[head_blk, k_blk, head_blk, col_blk, row_blk, col_blk, col_blk],
        out_specs=[out_blk, out_blk, out_blk, out_t_blk, out_blk],
        out_shape=[sds((r, HEAD_DIM), F32), sds((r, HEAD_DIM), BF16), sds((r, HEAD_DIM), BF16),
                   sds((HEAD_DIM, r), BF16), sds((r, HEAD_DIM), BF16)],
        compiler_params=_cparams(("arbitrary", "arbitrary")),
        name="dn_prep",
    )(qk, qk, v, gc_col, gc_row, tot_col, beta_col)


SCAN_ROWS = 256


SCAN_HEADS = 4


def _dn_scan_kernel(u0, u1, w0, w1, qd0, qd1, kdt0, kdt1, qk0, qk1, tot0, tot1, o0, o1, s_ref):
    @pl.when(pl.program_id(1) == 0)
    def _():
        s_ref[...] = jnp.zeros_like(s_ref)

    n_chunks = SCAN_ROWS // DN_CHUNK
    dirs = ((u0, w0, qd0, kdt0, qk0, tot0, o0), (u1, w1, qd1, kdt1, qk1, tot1, o1))
    chains = [(d, hh) for d in range(2) for hh in range(SCAN_HEADS)]
    s = {ch: s_ref[ch[0], ch[1]] for ch in chains}
    for step in range(n_chunks):
        cs = {0: step, 1: n_chunks - 1 - step}
        rs = {d: pl.ds(cs[d] * DN_CHUNK, DN_CHUNK) for d in range(2)}
        ws = {}
        for d, hh in chains:
            w_ref, qd_ref = dirs[d][1], dirs[d][2]
            wq = jnp.concatenate([w_ref[0, hh, rs[d], :], qd_ref[0, hh, rs[d], :]], axis=0)
            ws[d, hh] = _dot(wq, s[d, hh].astype(BF16))
        vnb = {(d, hh): (dirs[d][0][0, hh, rs[d], :] - ws[d, hh][:DN_CHUNK]).astype(BF16) for d, hh in chains}
        for d, hh in chains:
            kdt_ref, qk_ref, tot_ref, o_ref = dirs[d][3:]
            half = (cs[d] % 2) * DN_CHUNK
            qk = qk_ref[0, hh, rs[d], :][:, half:half + DN_CHUNK]
            o_ref[rs[d], hh * HEAD_DIM:(hh + 1) * HEAD_DIM] = ws[d, hh][DN_CHUNK:] + _dot(qk, vnb[d, hh])
            g_last = jnp.exp(tot_ref[0, hh, pl.ds(cs[d] * DN_CHUNK, 1), :])
            s[d, hh] = s[d, hh] * g_last + _dot(kdt_ref[0, hh, :, rs[d]], vnb[d, hh])
    for d, hh in chains:
        s_ref[d, hh] = s[d, hh]


def _dn_scan(u, w, qd, kdt, qk, tot_col, *, n_lat):
    r = u.shape[2]
    nlb = n_lat // SCAN_ROWS
    assert r - n_lat == SCAN_ROWS and N_HEADS % SCAN_HEADS == 0
    rb = (lambda t: jnp.where(t == 0, nlb, t - 1), lambda t: jnp.where(t == 0, nlb, nlb - t))

    def pair(arr, shape, transposed=False):
        specs = []
        for d in range(2):
            if transposed:
                imap = lambda h, t, d=d: (d, h, 0, rb[d](t))
            else:
                imap = lambda h, t, d=d: (d, h, rb[d](t), 0)
            specs.append(pl.BlockSpec((1, SCAN_HEADS) + shape, imap))
        return specs, [arr, arr]

    in_specs, args = [], []
    for arr, shape, tr in ((u, (SCAN_ROWS, HEAD_DIM), False), (w, (SCAN_ROWS, HEAD_DIM), False),
                           (qd, (SCAN_ROWS, HEAD_DIM), False), (kdt, (HEAD_DIM, SCAN_ROWS), True),
                           (qk, (SCAN_ROWS, HEAD_DIM), False), (tot_col, (SCAN_ROWS, 1), False)):
        sp, ar = pair(arr, shape, tr)
        in_specs += sp
        args += ar
    out_specs = [pl.BlockSpec((SCAN_ROWS, SCAN_HEADS * HEAD_DIM), lambda h, t, d=d: (rb[d](t), h)) for d in range(2)]
    return pl.pallas_call(
        _dn_scan_kernel,
        grid=(N_HEADS // SCAN_HEADS, nlb + 1),
        in_specs=in_specs,
        out_specs=out_specs,
        out_shape=[jax.ShapeDtypeStruct((r, DN_DIM), F32)] * 2,
        scratch_shapes=[pltpu.VMEM((2, SCAN_HEADS, HEAD_DIM, HEAD_DIM), F32)],
        compiler_params=_cparams(("arbitrary", "arbitrary")),
        name="dn_scan",
    )(*args)


def _dn_out_kernel(of_ref, ob_ref, gate_ref, ng_ref, y_ref):
    for h in range(N_HEADS):
        cs = slice(h * HEAD_DIM, (h + 1) * HEAD_DIM)
        o = of_ref[:, cs] + ob_ref[:, cs]
        o = o * lax.rsqrt(jnp.mean(o * o, axis=-1, keepdims=True) + 1e-6) * ng_ref[...]
        y_ref[:, cs] = (o * _silu(gate_ref[:, cs])).astype(y_ref.dtype)


def _dn_out(o_f, o_b, gate, norm_g):
    r, n = o_f.shape
    row = pl.BlockSpec((ELT_ROWS, n), lambda i: (i, 0))
    return pl.pallas_call(
        _dn_out_kernel,
        grid=(r // ELT_ROWS,),
        in_specs=[row, row, row, pl.BlockSpec((1, HEAD_DIM), lambda i: (0, 0))],
        out_specs=row,
        out_shape=jax.ShapeDtypeStruct((r, n), BF16),
        compiler_params=_cparams(("arbitrary",)),
        name="dn_out",
    )(o_f, o_b, gate, norm_g.astype(F32).reshape(1, HEAD_DIM))


S5_TG = LANES // S5_GROUP
S5_FOLD = S5_CHUNK * LANES
S5_SW = S5_TG * 2 * S5_STATE
S5_PAIR = 2 * SUBLANES


def _s5_tile_params(lam_re, lam_im, log_step, b_re, b_im, c_re, c_im):
    f = lambda a: a.astype(F32)
    lr, li, br, bi, cr, ci = map(f, (lam_re, lam_im, b_re, b_im, c_re, c_im))
    dt = jnp.exp(f(log_step))[..., None]
    mag, ang = jnp.exp(lr * dt), li * dt
    lbr, lbi = mag * jnp.cos(ang), mag * jnp.sin(ang)
    den = lr * lr + li * li
    zr = ((lbr - 1.0) * lr + lbi * li) / den
    zi = (lbi * lr - (lbr - 1.0) * li) / den
    bbr = zr[..., None] * br - zi[..., None] * bi
    bbi = zr[..., None] * bi + zi[..., None] * br
    g, p = lbr.shape[1], lbr.shape[2]
    nt = g // S5_TG
    tile_rows = lambda a: a.reshape(2, nt, LANES, p)
    per_row = lambda a: tile_rows(jnp.repeat(a, S5_GROUP, axis=1))
    cat = lambda a, b: jnp.concatenate([a, b], axis=-1)
    lrr = cat(per_row(lbr), per_row(lbr))
    lii = cat(-per_row(lbi), per_row(lbi))
    bbc = cat(tile_rows(jnp.swapaxes(bbr, 2, 3)), tile_rows(jnp.swapaxes(bbi, 2, 3)))
    cc = cat(tile_rows(cr), tile_rows(ci))

    def cmul(xr, xi, yr, yi):
        return xr * yr - xi * yi, xr * yi + xi * yr

    ar, ai = lbr, lbi
    for _ in range(int(math.log2(S5_CHUNK))):
        ar, ai = cmul(ar, ai, ar, ai)
    pw = [(ar, ai)]
    for _ in range(SUBLANES - 1):
        pw.append(cmul(*pw[-1], ar, ai))
    rows = lambda xr, xi: jnp.stack([cat(xr, xr), cat(-xi, xi)], axis=2)
    steps = jnp.stack([jnp.broadcast_to(rows(*pw[n - 1])[:, :, :, None, :], (2, g, 2, SUBLANES, 2 * p)) for n in (1, 2, 4)], axis=2)
    fwd = jnp.stack([rows(*pw[j])[0] for j in range(SUBLANES)], axis=2)
    bwd = jnp.stack([rows(*pw[SUBLANES - 1 - j])[1] for j in range(SUBLANES)], axis=2)
    consts = jnp.concatenate([steps, jnp.stack([fwd, bwd])[:, :, None]], axis=2)
    consts = consts.reshape(2, nt, S5_TG, 4, 2, SUBLANES, 2 * p).transpose(1, 0, 3, 4, 5, 2, 6)
    return lrr, lii, bbc, cc, consts.reshape(nt, 2, 4, 2, SUBLANES, S5_SW)


def _cmul_lanes(x, rr, ii):
    return x * rr + pltpu.roll(x, S5_STATE, 1) * ii


def _s5_ops_kernel(lrr_ref, lii_ref, bbc_ref, cc_ref, *out_refs, toeplitz):
    row = lax.broadcasted_iota(jnp.int32, (LANES, LANES), 0)
    lane = lax.broadcasted_iota(jnp.int32, (LANES, LANES), 1)
    same_group = (row // S5_GROUP) == (lane // S5_GROUP)
    conj = jnp.where(lane < S5_STATE, 1.0, -1.0)
    c = S5_CHUNK
    lag = []
    for d in range(2):
        rr, ii = lrr_ref[d, 0], lii_ref[d, 0]
        xs, zs = [bbc_ref[d, 0]], [cc_ref[d, 0]]
        for _ in range(c):
            xs.append(_cmul_lanes(xs[-1], rr, ii))
            zs.append(_cmul_lanes(zs[-1], rr, ii))
        if toeplitz:
            zc = zs[0] * conj
            z1, z2 = _split2(zc)
            ks = []
            for l in range(c):
                x1, x2 = _split2(xs[l])
                k = (lax.dot_general(x1, z1, _NT, preferred_element_type=F32)
                     + lax.dot_general(x1, z2, _NT, preferred_element_type=F32)
                     + lax.dot_general(x2, z1, _NT, preferred_element_type=F32))
                ks.append(jnp.where(same_group, k, 0.0))
            lag.append(ks)
        else:
            bc_ref, cct_ref = out_refs
            for i in range(c):
                xin = xs[c - 1 - i] if d == 0 else xs[i]
                zout = (zs[i + 1] if d == 0 else zs[c - i]) * conj
                rs = slice(i * LANES, (i + 1) * LANES)
                for t in range(S5_TG):
                    sel = (row // S5_GROUP) == t
                    cs = slice(t * LANES, (t + 1) * LANES)
                    bc_ref[d, 0, rs, cs] = jnp.where(sel, xin, 0.0).astype(BF16)
                    cct_ref[d, 0, rs, cs] = jnp.where(sel, zout, 0.0).astype(BF16)
    if toeplitz:
        (m_ref,) = out_refs
        fwd = [k.astype(BF16) for k in lag[0]]
        bwd = [k.astype(BF16) for k in lag[1]]
        diag = (lag[0][0] + lag[1][0]).astype(BF16)
        for i in range(c):
            for j in range(c):
                blk = diag if i == j else (fwd[j - i] if j > i else bwd[i - j])
                m_ref[0, i * LANES:(i + 1) * LANES, j * LANES:(j + 1) * LANES] = blk


def _s5_ops(lrr, lii, bbc, cc):
    nt = lrr.shape[1]
    par = pl.BlockSpec((2, 1, LANES, LANES), lambda q: (0, q, 0, 0))
    common = dict(grid=(nt,), in_specs=[par] * 4, compiler_params=_cparams(("arbitrary",)))
    m = pl.pallas_call(
        functools.partial(_s5_ops_kernel, toeplitz=True),
        out_specs=pl.BlockSpec((1, S5_FOLD, S5_FOLD), lambda q: (q, 0, 0)),
        out_shape=jax.ShapeDtypeStruct((nt, S5_FOLD, S5_FOLD), BF16),
        name="s5_ops_toeplitz", **common)(lrr, lii, bbc, cc)
    st = pl.BlockSpec((2, 1, S5_FOLD, S5_SW), lambda q: (0, q, 0, 0))
    bc, cct = pl.pallas_call(
        functools.partial(_s5_ops_kernel, toeplitz=False),
        out_specs=[st, st],
        out_shape=[jax.ShapeDtypeStruct((2, nt, S5_FOLD, S5_SW), BF16)] * 2,
        name="s5_ops_state", **common)(lrr, lii, bbc, cc)
    return m, bc, cct


def _s5_state_kernel(ut_ref, bc_ref, k_ref, e_ref, uf_ref, vf_ref, vb_ref, *, n_lat_c, n_ctx_c):
    n_all = n_lat_c + n_ctx_c
    for i in range(S5_CHUNK):
        uf_ref[0, :, i * LANES:(i + 1) * LANES] = ut_ref[0, pl.ds(i, n_all, stride=S5_CHUNK), :].astype(BF16)
    u = uf_ref[0]
    vf = _dot(u, bc_ref[0, 0])
    vf_ref[pl.ds(0, n_ctx_c), :] = vf[n_lat_c:]
    vf_ref[pl.ds(n_ctx_c, n_lat_c), :] = vf[:n_lat_c]
    vb_ref[...] = _dot(u, bc_ref[1, 0])
    sub = lax.broadcasted_iota(jnp.int32, (SUBLANES, LANES), 0)

    lanes = [(d, slice(g * LANES, (g + 1) * LANES)) for d in range(2) for g in range(S5_TG)]

    def tile_scans(xs, carry_rows):
        for n, sh in enumerate((1, 2, 4)):
            shifted = [jnp.where(sub >= sh, pltpu.roll(x, sh, 0), 0.0) if d == 0 else
                       jnp.where(sub < SUBLANES - sh, pltpu.roll(x, SUBLANES - sh, 0), 0.0)
                       for x, (d, _) in zip(xs, lanes)]
            xs = [x + _cmul_lanes(s, k_ref[0, d, n, 0, :, cs], k_ref[0, d, n, 1, :, cs])
                  for x, s, (d, cs) in zip(xs, shifted, lanes)]
        cbs = [jnp.broadcast_to(c, (SUBLANES, LANES)) for c in carry_rows]
        xs = [x + _cmul_lanes(cb, k_ref[0, d, 3, 0, :, cs], k_ref[0, d, 3, 1, :, cs])
              for x, cb, (d, cs) in zip(xs, cbs, lanes)]
        return [(jnp.where(sub == 0, cb, pltpu.roll(x, 1, 0)), x[SUBLANES - 1:SUBLANES, :]) if d == 0 else
                (jnp.where(sub == SUBLANES - 1, cb, pltpu.roll(x, SUBLANES - 1, 0)), x[0:1, :])
                for x, cb, (d, _) in zip(xs, cbs, lanes)]

    n_iter = n_all // S5_PAIR
    src = (vf_ref, vb_ref)

    def body(t, carries):
        base = (pl.multiple_of(t * S5_PAIR, S5_PAIR), pl.multiple_of((n_iter - 1 - t) * S5_PAIR, S5_PAIR))
        first = tile_scans([src[d][pl.ds(base[d] + d * SUBLANES, SUBLANES), cs] for d, cs in lanes], carries)
        second = tile_scans([src[d][pl.ds(base[d] + (1 - d) * SUBLANES, SUBLANES), cs] for d, cs in lanes],
                            [c for _, c in first])
        for (d, cs), (e1, _), (e2, _) in zip(lanes, first, second):
            lo, hi = (e1, e2) if d == 0 else (e2, e1)
            e_ref[d, 0, pl.ds(base[d], S5_PAIR), cs] = jnp.concatenate([lo, hi], axis=0).astype(BF16)
        return tuple(c for _, c in second)

    lax.fori_loop(0, n_iter, body, tuple(jnp.zeros((1, LANES), F32) for _ in lanes))


def _s5_out_kernel(uf_ref, m_ref, cct_ref, e_ref, y_ref, *, n_lat_c, n_ctx_c):
    y = _dot(uf_ref[0, pl.ds(0, n_lat_c), :], m_ref[0])
    y = y + lax.dot_general(e_ref[0, 0, pl.ds(n_ctx_c, n_lat_c), :], cct_ref[0, 0], _NT, preferred_element_type=F32)
    y = y + lax.dot_general(e_ref[1, 0, pl.ds(0, n_lat_c), :], cct_ref[1, 0], _NT, preferred_element_type=F32)
    first = pl.program_id(1) * (S5_SW // LANES)
    for ii in range(S5_SW // LANES):
        y_ref[0, pl.ds(first + ii, n_lat_c, stride=S5_CHUNK), :] = y[:, ii * LANES:(ii + 1) * LANES]


def _s5_scan(ut, m, bc, cct, consts, *, n_lat):
    nt, r, _ = ut.shape
    n_all = r // S5_CHUNK
    n_lat_c = n_lat // S5_CHUNK
    n_ctx_c = n_all - n_lat_c
    assert n_lat_c % S5_PAIR == 0 and n_ctx_c % S5_PAIR == 0
    dims = dict(n_lat_c=n_lat_c, n_ctx_c=n_ctx_c)
    e, uf = pl.pallas_call(
        functools.partial(_s5_state_kernel, **dims),
        grid=(nt,),
        in_specs=[
            pl.BlockSpec((1, r, LANES), lambda q: (q, 0, 0)),
            pl.BlockSpec((2, 1, S5_FOLD, S5_SW), lambda q: (0, q, 0, 0)),
            pl.BlockSpec((1, 2, 4, 2, SUBLANES, S5_SW), lambda q: (q, 0, 0, 0, 0, 0)),
        ],
        out_specs=[pl.BlockSpec((2, 1, n_all, S5_SW), lambda q: (0, q, 0, 0)),
                   pl.BlockSpec((1, n_all, S5_FOLD), lambda q: (q, 0, 0))],
        out_shape=[jax.ShapeDtypeStruct((2, nt, n_all, S5_SW), BF16),
                   jax.ShapeDtypeStruct((nt, n_all, S5_FOLD), BF16)],
        scratch_shapes=[pltpu.VMEM((n_all, S5_SW), F32), pltpu.VMEM((n_all, S5_SW), F32)],
        compiler_params=_cparams(("arbitrary",)),
        name="s5_state",
    )(ut, bc, consts)
    halves = S5_FOLD // S5_SW
    return pl.pallas_call(
        functools.partial(_s5_out_kernel, **dims),
        grid=(nt, halves),
        in_specs=[
            pl.BlockSpec((1, n_all, S5_FOLD), lambda q, j: (q, 0, 0)),
            pl.BlockSpec((1, S5_FOLD, S5_SW), lambda q, j: (q, 0, j)),
            pl.BlockSpec((2, 1, S5_SW, S5_SW), lambda q, j: (0, q, j, 0)),
            pl.BlockSpec((2, 1, n_all, S5_SW), lambda q, j: (0, q, 0, 0)),
        ],
        out_specs=pl.BlockSpec((1, n_lat, LANES), lambda q, j: (q, 0, 0)),
        out_shape=jax.ShapeDtypeStruct((nt, n_lat, LANES), F32),
        compiler_params=_cparams(("arbitrary", "arbitrary")),
        name="s5_out",
    )(uf, m, cct, e)


def _gelu_skip_kernel(y_ref, u_ref, d_ref, o_ref):
    for q in range(y_ref.shape[0]):
        cs = slice(q * LANES, (q + 1) * LANES)
        z = y_ref[q] + d_ref[:, cs] * u_ref[q]
        inner = math.sqrt(2.0 / math.pi) * (z + 0.044715 * (z * z * z))
        o_ref[:, cs] = (0.5 * z * (1.0 + jnp.tanh(inner))).astype(o_ref.dtype)


def _gelu_skip(y, u, d_skip):
    nt, r, _ = y.shape
    d = nt * LANES
    tile = pl.BlockSpec((nt, ELT_ROWS, LANES), lambda i: (0, i, 0))
    row = pl.BlockSpec((ELT_ROWS, d), lambda i: (i, 0))
    return pl.pallas_call(
        _gelu_skip_kernel,
        grid=(r // ELT_ROWS,),
        in_specs=[tile, tile, pl.BlockSpec((1, d), lambda i: (0, 0))],
        out_specs=row,
        out_shape=jax.ShapeDtypeStruct((r, d), BF16),
        compiler_params=_cparams(("arbitrary",)),
        name="gelu_skip",
    )(y, u, d_skip.astype(F32).reshape(1, d))


def _mixer_na_gdn(u, w_in, w_out, rpb, conv_w, a_log, dt_bias, norm_g, *, n_lat):
    r = u.shape[0]
    n_main = 3 * NA_DIM + 4 * DN_DIM
    qkv_na = _matmul(u, w_in, col0=0, ncols=3 * NA_DIM, tn=256, out_dtype=BF16)
    qkv_dn = _matmul(u, w_in, col0=3 * NA_DIM, ncols=3 * DN_DIM, tn=256, out_dtype=F32)
    gate = _matmul(u, w_in, col0=3 * NA_DIM + 3 * DN_DIM, ncols=DN_DIM, tn=256, out_dtype=F32)
    w_ab = jnp.pad(w_in[:, n_main:], ((0, 0), (0, LANES - 4 * N_HEADS)))
    ab = _matmul(u, w_ab, col0=0, ncols=LANES, tn=LANES, out_dtype=F32)

    o_na = _na_attention(qkv_na, _na_bias_table(rpb), n_lat=n_lat)

    cos, sin = _rope_tables(n_lat)
    qk = _dn_conv(qkv_dn, conv_w.astype(F32), cos, sin, n_lat=n_lat, col0=0, ncols=2 * DN_DIM, qk=True)
    v = _dn_conv(qkv_dn, conv_w.astype(F32), cos, sin, n_lat=n_lat, col0=2 * DN_DIM, ncols=DN_DIM, qk=False)
    gb = _dn_gates(ab, a_log, dt_bias)
    gb_t = gb.T.reshape(4, 2, N_HEADS, r)
    beta_col = gb_t[1].reshape(2, N_HEADS, r, 1)
    gc_col = gb_t[2].reshape(2, N_HEADS, r, 1)
    gc_row = gb_t[2].reshape(2, N_HEADS, 1, r)
    tot_col = gb_t[3].reshape(2, N_HEADS, r, 1)
    u_c, w_c, qd, kdt, qkm = _dn_prep(qk, v, gc_col, gc_row, tot_col, beta_col)
    o_f, o_b = _dn_scan(u_c, w_c, qd, kdt, qkm, tot_col, n_lat=n_lat)
    o_dn = _dn_out(o_f, o_b, gate, norm_g)
    return _matmul_cat(o_na, o_dn, w_out, tn=512, out_dtype=BF16)


def _mixer_s5(ut, lam_re, lam_im, log_step, b_re, b_im, c_re, c_im, d_skip, w_out, w_gate, *, n_lat):
    d = ut.shape[0] * LANES
    lrr, lii, bbc, cc, consts = _s5_tile_params(lam_re, lam_im, log_step, b_re, b_im, c_re, c_im)
    m, bc, cct = _s5_ops(lrr, lii, bbc, cc)
    y = _s5_scan(ut, m, bc, cct, consts, n_lat=n_lat)
    gz = _gelu_skip(y, ut, d_skip)
    return _matmul_pair(gz, w_out, w_gate, col1=0, col2=0, ncols=d, tn=256, gate_second=True, out_dtype=BF16)


def kernel(x, c, ctx, c_ctx, w_mod, b_mod, ln_g, ln_b, ffn_w_in, ffn_w_out, ab_w_in, ab_w_out, na_rpb, dn_conv_w, dn_a_log, dn_dt_bias, dn_norm_g, s5_lam_re, s5_lam_im, s5_log_step, s5_b_re, s5_b_im, s5_c_re, s5_c_im, s5_d, s5_w_out, s5_w_gate):
    depth, d = w_mod.shape[0], x.shape[2]
    assert depth == 2 and x.shape[0] == 1, "layer 0 is the NA/DeltaNet layer, layer 1 the final S5 layer"
    n_lat = x.shape[1]
    alpha = (2.0 * depth) ** 0.25
    h = jnp.concatenate([x[0], ctx[0]], axis=0)
    mods = _modulation(c, c_ctx, w_mod, b_mod)
    lng = ln_g.astype(F32).reshape(depth * 3, 1, d)
    lnb = ln_b.astype(F32).reshape(depth * 3, 1, d)
    res_ln = functools.partial(_res_ln, mods=mods, ln_g=lng, ln_b=lnb, alpha=alpha, n_lat=n_lat)
    w_out_bf = ffn_w_out.astype(BF16)
    ffn = lambda uu, l, s: _ffn(uu, ffn_w_in, w_out_bf, (l, s))

    u = _modulate(h, mods, layer=0, sub=0, n_lat=n_lat)
    h, u = res_ln(h, ffn(u, 0, 0), layer=0, sub=0, weight=MACARON_WEIGHT, next_mod=(0, 1))
    y = _mixer_na_gdn(u, ab_w_in[0], ab_w_out[0], na_rpb[0], dn_conv_w[0], dn_a_log[0], dn_dt_bias[0], dn_norm_g[0],
                      n_lat=n_lat)
    h, u = res_ln(h, y, layer=0, sub=1, weight=1.0, next_mod=(0, 2))
    h, u = res_ln(h, ffn(u, 0, 1), layer=0, sub=2, weight=MACARON_WEIGHT, next_mod=(1, 0))

    h, ut = res_ln(h, ffn(u, 1, 0), layer=1, sub=0, weight=MACARON_WEIGHT, next_mod=(1, 1), u_dtypes=(F32,), u_tiles=True)
    y = _mixer_s5(ut, s5_lam_re[0], s5_lam_im[0], s5_log_step[0], s5_b_re[0], s5_b_im[0], s5_c_re[0], s5_c_im[0],
                  s5_d[0], s5_w_out[0], s5_w_gate[0], n_lat=n_lat)
    h, u = res_ln(h, y, layer=1, sub=1, weight=1.0, next_mod=(1, 2))
    (out,) = res_ln(h, ffn(u, 1, 1), layer=1, sub=2, weight=MACARON_WEIGHT, next_mod=None)
    return out[None]
```

```python
import functools
import math

import numpy as np
import jax
import jax.numpy as jnp
from jax import lax
from jax.experimental import pallas as pl
from jax.experimental.pallas import tpu as pltpu

GRID_W = 64
HEAD_DIM = 128
N_HEADS = 16
NA_DIM = N_HEADS * HEAD_DIM
DN_DIM = N_HEADS * HEAD_DIM
NA_WIN_ROWS = 8
NA_WIN_COLS = 16
DN_CONV = 5
DN_CHUNK = 64
ROPE_BASE = 10000.0
S5_GROUP = 16
S5_STATE = 64
S5_CHUNK = 16
N_MOD = 9
LN_EPS = 1e-5
MACARON_WEIGHT = 0.5

V7X_VMEM_BYTES = 64 * 1024 * 1024
VMEM_LIMIT = V7X_VMEM_BYTES - 8 * 1024 * 1024
LANES = 128
SUBLANES = 8

F32 = jnp.float32
BF16 = jnp.bfloat16


def _cparams(sem):
    return pltpu.CompilerParams(dimension_semantics=sem, vmem_limit_bytes=VMEM_LIMIT)


def _silu(x):
    return x * (1.0 / (1.0 + jnp.exp(-x)))


def _sigmoid(x):
    return 1.0 / (1.0 + jnp.exp(-x))


def _row_tile(rows, candidates=(768, 512, 256)):
    for t in candidates:
        if rows % t == 0:
            return t
    raise ValueError(f"unsupported row count {rows}")


F32_WEIGHT_ROW_TILES = (1408, 1024, 768, 512, 256)


def _weight_row_tile(rows, w):
    return _row_tile(rows, F32_WEIGHT_ROW_TILES if w.dtype == F32 else (768, 512, 256))


def _weight_spec(lead, k, tn, col_block0, n_grid):
    if n_grid == 2:
        imap = lambda i, j: lead + (0, j + col_block0)
    else:
        imap = lambda i, j, kk: lead + (kk, j + col_block0)
    return pl.BlockSpec((None,) * len(lead) + (k, tn), imap)


def _mod_kernel(c_ref, w_ref, b_ref, o_ref):
    s = _silu(c_ref[...]).astype(BF16)
    w = w_ref[0].astype(BF16)
    o_ref[0] = jnp.dot(s, w, preferred_element_type=F32) + b_ref[0]


def _modulation(c, c_ctx, w_mod, b_mod):
    depth, d, n = w_mod.shape
    tn = 512
    cc = jnp.zeros((SUBLANES, d), F32).at[0].set(c[0]).at[1].set(c_ctx)
    out = pl.pallas_call(
        _mod_kernel,
        grid=(depth, n // tn),
        in_specs=[
            pl.BlockSpec((SUBLANES, d), lambda l, j: (0, 0)),
            pl.BlockSpec((1, d, tn), lambda l, j: (l, 0, j)),
            pl.BlockSpec((1, 1, tn), lambda l, j: (l, 0, j)),
        ],
        out_specs=pl.BlockSpec((1, SUBLANES, tn), lambda l, j: (l, 0, j)),
        out_shape=jax.ShapeDtypeStruct((depth, SUBLANES, n), F32),
        compiler_params=_cparams(("arbitrary", "arbitrary")),
        name="modulation",
    )(cc, w_mod, b_mod.reshape(depth, 1, n))
    return out[:, :2].reshape(depth * 2 * N_MOD, 1, d)


def _mm_kernel(a_ref, w_ref, o_ref):
    o_ref[...] = jnp.dot(a_ref[...], w_ref[...].astype(BF16), preferred_element_type=F32).astype(o_ref.dtype)


def _matmul(a, w, *, lead=(), col0, ncols, tn, out_dtype):
    r, k = a.shape
    tm = _weight_row_tile(r, w)
    assert col0 % tn == 0 and ncols % tn == 0 and w.shape[len(lead)] == k
    return pl.pallas_call(
        _mm_kernel,
        grid=(r // tm, ncols // tn),
        in_specs=[pl.BlockSpec((tm, k), lambda i, j: (i, 0)), _weight_spec(lead, k, tn, col0 // tn, 2)],
        out_specs=pl.BlockSpec((tm, tn), lambda i, j: (i, j)),
        out_shape=jax.ShapeDtypeStruct((r, ncols), out_dtype),
        compiler_params=_cparams(("arbitrary", "arbitrary")),
        name="matmul",
    )(a, w)


def _mm_pair_kernel(a_ref, w1_ref, w2_ref, o_ref, *, gate_second):
    a = a_ref[...]
    p1 = jnp.dot(a, w1_ref[...].astype(BF16), preferred_element_type=F32)
    p2 = jnp.dot(a, w2_ref[...].astype(BF16), preferred_element_type=F32)
    if gate_second:
        o = p1 * _sigmoid(p2)
    else:
        o = _silu(p1) * p2
    o_ref[...] = o.astype(o_ref.dtype)


def _matmul_pair(a, w1, w2, *, lead=(), col1, col2, ncols, tn, gate_second, out_dtype):
    r, k = a.shape
    tm = _weight_row_tile(r, w1)
    assert col1 % tn == 0 and col2 % tn == 0 and ncols % tn == 0
    return pl.pallas_call(
        functools.partial(_mm_pair_kernel, gate_second=gate_second),
        grid=(r // tm, ncols // tn),
        in_specs=[pl.BlockSpec((tm, k), lambda i, j: (i, 0)),
                  _weight_spec(lead, k, tn, col1 // tn, 2), _weight_spec(lead, k, tn, col2 // tn, 2)],
        out_specs=pl.BlockSpec((tm, tn), lambda i, j: (i, j)),
        out_shape=jax.ShapeDtypeStruct((r, ncols), out_dtype),
        compiler_params=_cparams(("arbitrary", "arbitrary")),
        name="matmul_pair",
    )(a, w1, w2)


def _mm_cat_kernel(a1_ref, a2_ref, w_ref, o_ref, acc_ref):
    k = pl.program_id(2)

    @pl.when(k == 0)
    def _():
        acc_ref[...] = jnp.dot(a1_ref[...], w_ref[...].astype(BF16), preferred_element_type=F32)

    @pl.when(k == 1)
    def _():
        o_ref[...] = (acc_ref[...]
                      + jnp.dot(a2_ref[...], w_ref[...].astype(BF16), preferred_element_type=F32)).astype(o_ref.dtype)


def _matmul_cat(a1, a2, w, *, tn, out_dtype):
    r, k1 = a1.shape
    assert a2.shape == (r, k1) and w.shape[0] == 2 * k1
    n = w.shape[1]
    tm = _weight_row_tile(r, w)
    return pl.pallas_call(
        _mm_cat_kernel,
        grid=(r // tm, n // tn, 2),
        in_specs=[
            pl.BlockSpec((tm, k1), lambda i, j, k: (i, 0)),
            pl.BlockSpec((tm, k1), lambda i, j, k: (i, 0)),
            _weight_spec((), k1, tn, 0, 3),
        ],
        out_specs=pl.BlockSpec((tm, tn), lambda i, j, k: (i, j)),
        out_shape=jax.ShapeDtypeStruct((r, n), out_dtype),
        scratch_shapes=[pltpu.VMEM((tm, tn), F32)],
        compiler_params=_cparams(("arbitrary", "arbitrary", "arbitrary")),
        name="matmul_cat",
    )(a1, a2, w)


ELT_ROWS = 256


def _modulate_kernel(z_ref, shift_ref, scale_ref, u_ref):
    u_ref[...] = (z_ref[...] * (1.0 + scale_ref[0]) + shift_ref[0]).astype(u_ref.dtype)


def _mod_row(layer, r, n_lat_tiles):
    return lambda i: ((layer * 2 + jnp.where(i >= n_lat_tiles, 1, 0)) * N_MOD + r, 0, 0)


def _modulate(z, mods, *, layer, sub, n_lat):
    r, d = z.shape
    nlt = n_lat // ELT_ROWS
    vec = lambda row: pl.BlockSpec((1, 1, d), _mod_row(layer, row, nlt))
    return pl.pallas_call(
        _modulate_kernel,
        grid=(r // ELT_ROWS,),
        in_specs=[pl.BlockSpec((ELT_ROWS, d), lambda i: (i, 0)), vec(3 * sub), vec(3 * sub + 1)],
        out_specs=pl.BlockSpec((ELT_ROWS, d), lambda i: (i, 0)),
        out_shape=jax.ShapeDtypeStruct((r, d), BF16),
        compiler_params=_cparams(("arbitrary",)),
        name="modulate",
    )(z, mods, mods)


def _res_ln_kernel(*refs, alpha, weight, n_u):
    z_ref, y_ref, gate_ref, g_ref, b_ref = refs[:5]
    t = alpha * z_ref[...] + (weight * gate_ref[0]) * y_ref[...].astype(F32)
    mu = jnp.mean(t, axis=-1, keepdims=True)
    tc = t - mu
    var = jnp.mean(tc * tc, axis=-1, keepdims=True)
    x = tc * lax.rsqrt(var + LN_EPS) * g_ref[0] + b_ref[0]
    if n_u == 0:
        refs[5][...] = x
        return
    shift_ref, scale_ref, x_ref = refs[5], refs[6], refs[7]
    x_ref[...] = x
    u = x * (1.0 + scale_ref[0]) + shift_ref[0]
    for u_ref in refs[8:]:
        if len(u_ref.shape) == 3:
            for q in range(u_ref.shape[0]):
                u_ref[q] = u[:, q * LANES:(q + 1) * LANES].astype(u_ref.dtype)
        else:
            u_ref[...] = u.astype(u_ref.dtype)


def _res_ln(z, y, *, mods, ln_g, ln_b, layer, sub, weight, alpha, n_lat, next_mod, u_dtypes=(BF16,), u_tiles=False):
    r, d = y.shape
    nlt = n_lat // ELT_ROWS
    vec = lambda lyr, row: pl.BlockSpec((1, 1, d), _mod_row(lyr, row, nlt))
    ln_idx = layer * 3 + sub
    ln_spec = pl.BlockSpec((1, 1, d), lambda i: (ln_idx, 0, 0))
    row_spec = pl.BlockSpec((ELT_ROWS, d), lambda i: (i, 0))
    in_specs = [row_spec, row_spec, vec(layer, 3 * sub + 2), ln_spec, ln_spec]
    args = [z, y, mods, ln_g, ln_b]
    out_specs = [row_spec]
    out_shape = [jax.ShapeDtypeStruct((r, d), F32)]
    n_u = 0
    if next_mod is not None:
        nl, ns = next_mod
        in_specs += [vec(nl, 3 * ns), vec(nl, 3 * ns + 1)]
        args += [mods, mods]
        n_u = len(u_dtypes)
        if u_tiles:
            out_specs += [pl.BlockSpec((d // LANES, ELT_ROWS, LANES), lambda i: (0, i, 0))] * n_u
            out_shape += [jax.ShapeDtypeStruct((d // LANES, r, LANES), dt) for dt in u_dtypes]
        else:
            out_specs += [row_spec] * n_u
            out_shape += [jax.ShapeDtypeStruct((r, d), dt) for dt in u_dtypes]
    res = pl.pallas_call(
        functools.partial(_res_ln_kernel, alpha=alpha, weight=weight, n_u=n_u),
        grid=(r // ELT_ROWS,),
        in_specs=in_specs,
        out_specs=out_specs,
        out_shape=out_shape,
        compiler_params=_cparams(("arbitrary",)),
        name="res_ln",
    )(*args)
    return res


def _ffn(u, w_in, w_out, lead):
    d_ff, d = w_out.shape[-2:]
    hmid = _matmul_pair(u, w_in, w_in, lead=lead, col1=0, col2=d_ff, ncols=d_ff, tn=256, gate_second=False,
                        out_dtype=BF16)
    return _matmul(hmid, w_out, lead=lead, col0=0, ncols=d, tn=512, out_dtype=BF16)


NEG_BIG = -1e30
NA_UNROLL = 16
_NT =(((1,), (1,)), ((), ()))


def _na_bias_table(rpb):
    kr, kw, w = NA_WIN_ROWS, NA_WIN_COLS, GRID_W
    pat = np.arange(kr)[:, None]
    i = np.arange(kr)[None, :]
    ridx = i - pat + (NA_WIN_ROWS - 1)
    col = np.arange(w)
    cs = np.clip(col - kw // 2, 0, w - kw)
    c2 = np.arange(w)[None, :]
    valid = (c2 >= cs[:, None]) & (c2 < cs[:, None] + kw)
    cidx = np.clip(c2 - col[:, None] + (NA_WIN_COLS - 1), 0, 2 * NA_WIN_COLS - 2)
    row_sel = (ridx[:, :, None] == np.arange(2 * kr - 1)).astype(np.float32)
    col_sel = ((cidx[:, :, None] == np.arange(2 * kw - 1)) & valid[:, :, None]).astype(np.float32)
    by_row = jnp.einsum('pia,hab->hpib', jnp.asarray(row_sel), rpb.astype(F32), precision=lax.Precision.HIGHEST)
    tbl = jnp.einsum('hpib,cdb->hpcid', by_row, jnp.asarray(col_sel), precision=lax.Precision.HIGHEST)
    tbl = tbl + jnp.asarray(np.where(valid, 0.0, NEG_BIG).astype(np.float32))[None, None, :, None, :]
    return tbl.reshape(rpb.shape[0], kr, w, kr * w)


def _na_kernel(q_ref, k_ref, v_ref, bias_ref, o_ref, *, n_lat, n_ctx):
    rows = n_lat // GRID_W
    kr = NA_WIN_ROWS
    scale = HEAD_DIM ** -0.5
    kc = k_ref[pl.ds(n_lat, n_ctx), :]
    vc = v_ref[pl.ds(n_lat, n_ctx), :]

    def attend(s_parts, v_parts):
        m = functools.reduce(jnp.maximum, [jnp.max(s, axis=-1, keepdims=True) for s in s_parts])
        ps = [jnp.exp(s - m) for s in s_parts]
        den = functools.reduce(lambda a, b: a + b, [jnp.sum(p, axis=-1, keepdims=True) for p in ps])
        num = functools.reduce(lambda a, b: a + b,
                               [jnp.dot(p.astype(BF16), v, preferred_element_type=F32) for p, v in zip(ps, v_parts)])
        return num / den

    def body(it, carry):
        rs = [it * NA_UNROLL + n for n in range(NA_UNROLL)]
        r0s = [jnp.clip(r - kr // 2, 0, rows - kr) for r in rs]
        qrows = [pl.ds(pl.multiple_of(r * GRID_W, GRID_W), GRID_W) for r in rs]
        krows = [pl.ds(pl.multiple_of(r0 * GRID_W, GRID_W), kr * GRID_W) for r0 in r0s]
        qs = [q_ref[qr, :] for qr in qrows]
        s_loc = [lax.dot_general(q, k_ref[kr_, :], _NT, preferred_element_type=F32) * scale + bias_ref[0, r - r0]
                 for q, kr_, r, r0 in zip(qs, krows, rs, r0s)]
        s_ctx = [lax.dot_general(q, kc, _NT, preferred_element_type=F32) * scale for q in qs]
        outs = [attend([sl, sc], [v_ref[kr_, :], vc]) for sl, sc, kr_ in zip(s_loc, s_ctx, krows)]
        for qr, o in zip(qrows, outs):
            o_ref[qr, :] = o.astype(o_ref.dtype)
        return carry

    assert rows % NA_UNROLL == 0
    lax.fori_loop(0, rows // NA_UNROLL, body, 0)
    qc = q_ref[pl.ds(n_lat, n_ctx), :]
    s_cc = lax.dot_general(qc, kc, _NT, preferred_element_type=F32) * scale
    o_ref[pl.ds(n_lat, n_ctx), :] = attend([s_cc], [vc]).astype(o_ref.dtype)


def _na_attention(qkv, bias_tbl, *, n_lat):
    r = qkv.shape[0]
    n_ctx = r - n_lat
    assert n_lat // GRID_W >= NA_WIN_ROWS
    blk = lambda off: pl.BlockSpec((r, HEAD_DIM), lambda h: (0, h + off))
    return pl.pallas_call(
        functools.partial(_na_kernel, n_lat=n_lat, n_ctx=n_ctx),
        grid=(N_HEADS,),
        in_specs=[blk(0), blk(N_HEADS), blk(2 * N_HEADS),
                  pl.BlockSpec((1,) + bias_tbl.shape[1:], lambda h: (h, 0, 0, 0))],
        out_specs=pl.BlockSpec((r, HEAD_DIM), lambda h: (0, h)),
        out_shape=jax.ShapeDtypeStruct((r, NA_DIM), BF16),
        compiler_params=_cparams(("arbitrary",)),
        name="na_attention",
    )(qkv, qkv, qkv, bias_tbl)


CONV_ROWS = 256
PAD = SUBLANES


def _rope_tables(n_lat):
    t = np.arange(n_lat)
    n_freq = HEAD_DIM // 4
    inv = ROPE_BASE ** (-np.arange(n_freq, dtype=np.float32) / n_freq)
    ang = np.concatenate([(t // GRID_W).astype(np.float32)[:, None] * inv,
                          (t % GRID_W).astype(np.float32)[:, None] * inv], -1).astype(np.float32)
    ang = jnp.asarray(ang)
    cos = jnp.repeat(jnp.cos(ang), 2, axis=-1)
    sin = jnp.repeat(jnp.sin(ang), 2, axis=-1) * jnp.asarray(np.tile(np.array([-1.0, 1.0], np.float32), HEAD_DIM // 2))
    return cos, sin


def _dn_conv_kernel(x_ref, w_ref, cos_ref, sin_ref, o_ref, xp_ref, *, n_lat, n_ctx, qk):
    half = DN_CONV // 2
    zeros = jnp.zeros((PAD, LANES), F32)
    xp_ref[pl.ds(0, PAD), :] = zeros
    xp_ref[pl.ds(PAD, n_lat), :] = x_ref[pl.ds(0, n_lat), :]
    xp_ref[pl.ds(PAD + n_lat, PAD), :] = zeros
    xp_ref[pl.ds(2 * PAD + n_lat, n_ctx), :] = x_ref[pl.ds(n_lat, n_ctx), :]
    xp_ref[pl.ds(2 * PAD + n_lat + n_ctx, PAD), :] = zeros
    w = w_ref[...]
    lane = lax.broadcasted_iota(jnp.int32, (CONV_ROWS, LANES), 1)
    even = (lane % 2) == 0

    def chunk(src0, dst0, rope_row0):
        big = xp_ref[pl.ds(src0 - PAD, CONV_ROWS + 2 * PAD), :]
        acc = jnp.zeros((CONV_ROWS, LANES), F32)
        for j in range(DN_CONV):
            acc = acc + big[PAD + j - half: PAD + j - half + CONV_ROWS, :] * w[j:j + 1, :]
        y = _silu(acc)
        if qk:
            y = y * lax.rsqrt(jnp.sum(y * y, axis=-1, keepdims=True) + 1e-6)
            if rope_row0 is not None:
                swapped = jnp.where(even, pltpu.roll(y, LANES - 1, 1), pltpu.roll(y, 1, 1))
                y = y * cos_ref[pl.ds(rope_row0, CONV_ROWS), :] + swapped * sin_ref[pl.ds(rope_row0, CONV_ROWS), :]
        o_ref[pl.ds(dst0, CONV_ROWS), :] = y

    def lat_body(i, carry):
        r0 = pl.multiple_of(i * CONV_ROWS, CONV_ROWS)
        chunk(r0 + PAD, r0, r0)
        return carry

    lax.fori_loop(0, n_lat // CONV_ROWS, lat_body, 0)
    for i in range(n_ctx // CONV_ROWS):
        chunk(2 * PAD + n_lat + i * CONV_ROWS, n_lat + i * CONV_ROWS, None)


def _dn_conv(qkv, conv_w, cos, sin, *, n_lat, col0, ncols, qk):
    r = qkv.shape[0]
    n_ctx = r - n_lat
    assert n_lat % CONV_ROWS == 0 and n_ctx % CONV_ROWS == 0
    jb = col0 // LANES
    return pl.pallas_call(
        functools.partial(_dn_conv_kernel, n_lat=n_lat, n_ctx=n_ctx, qk=qk),
        grid=(ncols // LANES,),
        in_specs=[
            pl.BlockSpec((r, LANES), lambda j: (0, j + jb)),
            pl.BlockSpec((DN_CONV, LANES), lambda j: (0, j + jb)),
            pl.BlockSpec((n_lat, LANES), lambda j: (0, 0)),
            pl.BlockSpec((n_lat, LANES), lambda j: (0, 0)),
        ],
        out_specs=pl.BlockSpec((r, LANES), lambda j: (0, j)),
        out_shape=jax.ShapeDtypeStruct((r, ncols), F32),
        scratch_shapes=[pltpu.VMEM((r + 3 * PAD, LANES), F32)],
        compiler_params=_cparams(("arbitrary",)),
        name="dn_conv",
    )(qkv, conv_w, cos, sin)


def _dn_gates_kernel(ab_ref, alog_ref, dtb_ref, o_ref):
    ab = ab_ref[...]
    z = ab + dtb_ref[...]
    softplus = jnp.maximum(z, 0.0) + jnp.log(1.0 + jnp.exp(-jnp.abs(z)))
    lane = lax.broadcasted_iota(jnp.int32, ab.shape, 1)
    g = jnp.where(lane < 2 * N_HEADS, -jnp.exp(alog_ref[...]) * softplus, 0.0)
    n = ab.shape[0]
    i = lax.broadcasted_iota(jnp.int32, (n, n), 0)
    j = lax.broadcasted_iota(jnp.int32, (n, n), 1)
    same = (i // DN_CHUNK) == (j // DN_CHUNK)
    one = lambda m: jnp.where(m, 1.0, 0.0).astype(BF16)
    gc = jnp.where(lane < N_HEADS, _dot_exact_left(one(same & (i >= j)), g), _dot_exact_left(one(same & (i <= j)), g))
    tot = _dot_exact_left(one(same), g)
    quarter = LANES // 4
    o_ref[...] = jnp.where(lane < quarter, g,
                           jnp.where(lane < 2 * quarter, _sigmoid(ab),
                                     jnp.where(lane < 3 * quarter, pltpu.roll(gc, 2 * quarter, 1),
                                               pltpu.roll(tot, 3 * quarter, 1))))


def _dn_gates(ab, a_log, dt_bias):
    r = ab.shape[0]
    pad = lambda v: jnp.zeros((1, LANES), F32).at[0, :2 * N_HEADS].set(v.astype(F32).reshape(-1))
    return pl.pallas_call(
        _dn_gates_kernel,
        grid=(r // ELT_ROWS,),
        in_specs=[pl.BlockSpec((ELT_ROWS, LANES), lambda i: (i, 0)),
                  pl.BlockSpec((1, LANES), lambda i: (0, 0)),
                  pl.BlockSpec((1, LANES), lambda i: (0, 0))],
        out_specs=pl.BlockSpec((ELT_ROWS, LANES), lambda i: (i, 0)),
        out_shape=jax.ShapeDtypeStruct((r, LANES), F32),
        compiler_params=_cparams(("arbitrary",)),
        name="dn_gates",
    )(ab, pad(a_log), pad(dt_bias))


BLK = 2 * DN_CHUNK


def _split3(x):
    x1 = x.astype(BF16)
    r1 = x - x1.astype(F32)
    x2 = r1.astype(BF16)
    x3 = (r1 - x2.astype(F32)).astype(BF16)
    return x1, x2, x3


def _dot(a, b):
    return jnp.dot(a, b, preferred_element_type=F32)


def _dot_exact_left(m01, x):
    x1, x2, x3 = _split3(x)
    return _dot(m01, x1) + _dot(m01, x2) + _dot(m01, x3)


def _split2(x):
    x1 = x.astype(BF16)
    return x1, (x - x1.astype(F32)).astype(BF16)


def _dn_chunk_math(units):
    n = BLK
    i = lax.broadcasted_iota(jnp.int32, (n, n), 0)
    j = lax.broadcasted_iota(jnp.int32, (n, n), 1)
    same = (i // DN_CHUNK) == (j // DN_CHUNK)
    masks = {False: (same & (i >= j), same & (i > j)), True: (same & (i <= j), same & (i < j))}
    eye = jnp.where(i == j, 1.0, 0.0)
    each = lambda fn, *lists: [fn(*xs) for xs in zip(*lists)]
    q, k, v, gc_col, gc_row, tot_col, beta_col, rev = map(list, zip(*units))
    incl = [masks[r][0] for r in rev]
    strict = [masks[r][1] for r in rev]
    decay = each(lambda m, c, r: jnp.where(m, jnp.exp(jnp.where(m, c - r, 0.0)), 0.0), incl, gc_col, gc_row)
    kb = each(lambda a, b: a * b, k, beta_col)
    qs = [x * (HEAD_DIM ** -0.5) for x in q]
    kk_qk = each(lambda a, b, c: lax.dot_general(jnp.concatenate([a, b], axis=0).astype(BF16), c.astype(BF16), _NT,
                                                 preferred_element_type=F32), kb, qs, k)
    a = each(lambda m, x, dc: jnp.where(m, x[:n] * dc, 0.0), strict, kk_qk, decay)
    qk = each(lambda m, x, dc: jnp.where(m, x[n:] * dc, 0.0), incl, kk_qk, decay)
    t = [eye - x for x in a]
    ab = [x.astype(BF16) for x in a]
    p = [_dot(x, x) for x in ab]
    levels = int(math.log2(DN_CHUNK)) - 1
    for lvl in range(levels):
        last = lvl == levels - 1
        prod = [_dot((tt if last else jnp.concatenate([tt, pp], axis=0)).astype(BF16), pp.astype(BF16))
                for tt, pp in zip(t, p)]
        t = [tt + pr[:n] for tt, pr in zip(t, prod)]
        if not last:
            p = [pr[n:] for pr in prod]
    eg = [jnp.exp(c) for c in gc_col]
    rhs = each(lambda vv, b, kk, e: jnp.concatenate([(vv * b).astype(BF16), (kk * e).astype(BF16)], axis=1),
               v, beta_col, kb, eg)
    uw = [_dot(tt.astype(BF16), r) for tt, r in zip(t, rhs)]
    kdt = each(lambda kk, tc, c: (kk * jnp.exp(tc - c)).T.astype(BF16), k, tot_col, gc_col)
    return [(x[:, :HEAD_DIM], x[:, HEAD_DIM:].astype(BF16), (qq * e).astype(BF16), kt, m.astype(BF16))
            for x, qq, e, kt, m in zip(uw, qs, eg, kdt, qk)]


PREP_BLOCKS = 2
PREP_HEADS = 2


def _dn_prep_kernel(q_ref, k_ref, v_ref, gc_ref, tot_ref, beta_ref,
                    u_ref, w_ref, qd_ref, kdt_ref, qk_ref):
    where = [(d, hh, pl.ds(b * BLK, BLK)) for b in range(PREP_BLOCKS) for hh in range(PREP_HEADS) for d in range(2)]
    hcols = lambda hh: slice(hh * HEAD_DIM, (hh + 1) * HEAD_DIM)
    col = lambda ref, d, hh, rs: jnp.broadcast_to(ref[d, hh, :, rs], (BLK, BLK)).T
    units = [(q_ref[rs, hcols(hh)], k_ref[rs, hcols(hh)], v_ref[rs, hcols(hh)], col(gc_ref, d, hh, rs),
              gc_ref[d, hh, :, rs], col(tot_ref, d, hh, rs), col(beta_ref, d, hh, rs), d == 1) for d, hh, rs in where]
    for (d, hh, rs), (u, w, qd, kdt, qk) in zip(where, _dn_chunk_math(units)):
        u_ref[d, hh, rs, :] = u
        w_ref[d, hh, rs, :] = w
        qd_ref[d, hh, rs, :] = qd
        kdt_ref[d, hh, :, rs] = kdt
        qk_ref[d, hh, rs, :] = qk


def _dn_prep(qk, v, gc_row, tot_row, beta_row):
    r = v.shape[0]
    rb = PREP_BLOCKS * BLK
    assert r % rb == 0 and BLK == HEAD_DIM
    ph = PREP_HEADS
    head_blk = pl.BlockSpec((rb, ph * HEAD_DIM), lambda h, t: (t, h))
    k_blk = pl.BlockSpec((rb, ph * HEAD_DIM), lambda h, t: (t, h + N_HEADS // ph))
    row_blk = pl.BlockSpec((2, ph, 1, rb), lambda h, t: (0, h, 0, t))
    out_blk = pl.BlockSpec((2, ph, rb, HEAD_DIM), lambda h, t: (0, h, t, 0))
    out_t_blk = pl.BlockSpec((2, ph, HEAD_DIM, rb), lambda h, t: (0, h, 0, t))
    sds = lambda shape, dt: jax.ShapeDtypeStruct((2, N_HEADS) + shape, dt)
    return pl.pallas_call(
        _dn_prep_kernel,
        grid=(N_HEADS // ph, r // rb),
        in_specs=[head_blk, k_blk, head_blk, row_blk, row_blk, row_blk],
        out_specs=[out_blk, out_blk, out_blk, out_t_blk, out_blk],
        out_shape=[sds((r, HEAD_DIM), F32), sds((r, HEAD_DIM), BF16), sds((r, HEAD_DIM), BF16),
                   sds((HEAD_DIM, r), BF16), sds((r, HEAD_DIM), BF16)],
        compiler_params=_cparams(("arbitrary", "arbitrary")),
        name="dn_prep",
    )(qk, qk, v, gc_row, tot_row, beta_row)


SCAN_ROWS = 256


SCAN_HEADS = 4


def _dn_scan_kernel(u0, u1, w0, w1, qd0, qd1, kdt0, kdt1, qk0, qk1, tot0, tot1, o0, o1, s_ref):
    @pl.when(pl.program_id(1) == 0)
    def _():
        s_ref[...] = jnp.zeros_like(s_ref)

    n_chunks = SCAN_ROWS // DN_CHUNK
    dirs = ((u0, w0, qd0, kdt0, qk0, tot0, o0), (u1, w1, qd1, kdt1, qk1, tot1, o1))
    chains = [(d, hh) for d in range(2) for hh in range(SCAN_HEADS)]
    s = {ch: s_ref[ch[0], ch[1]] for ch in chains}
    for step in range(n_chunks):
        cs = {0: step, 1: n_chunks - 1 - step}
        rs = {d: pl.ds(cs[d] * DN_CHUNK, DN_CHUNK) for d in range(2)}
        ws = {}
        for d, hh in chains:
            w_ref, qd_ref = dirs[d][1], dirs[d][2]
            wq = jnp.concatenate([w_ref[0, hh, rs[d], :], qd_ref[0, hh, rs[d], :]], axis=0)
            ws[d, hh] = _dot(wq, s[d, hh].astype(BF16))
        vnb = {(d, hh): (dirs[d][0][0, hh, rs[d], :] - ws[d, hh][:DN_CHUNK]).astype(BF16) for d, hh in chains}
        for d, hh in chains:
            kdt_ref, qk_ref, tot_ref, o_ref = dirs[d][3:]
            half = (cs[d] % 2) * DN_CHUNK
            qk = qk_ref[0, hh, rs[d], :][:, half:half + DN_CHUNK]
            o_ref[rs[d], hh * HEAD_DIM:(hh + 1) * HEAD_DIM] = ws[d, hh][DN_CHUNK:] + _dot(qk, vnb[d, hh])
            g_last = jnp.exp(tot_ref[0, hh, :, pl.ds(cs[d] * DN_CHUNK, 1)])
            s[d, hh] = s[d, hh] * g_last + _dot(kdt_ref[0, hh, :, rs[d]], vnb[d, hh])
    for d, hh in chains:
        s_ref[d, hh] = s[d, hh]


def _dn_scan(u, w, qd, kdt, qk, tot_row, *, n_lat):
    r = u.shape[2]
    nlb = n_lat // SCAN_ROWS
    assert r - n_lat == SCAN_ROWS and N_HEADS % SCAN_HEADS == 0
    rb = (lambda t: jnp.where(t == 0, nlb, t - 1), lambda t: jnp.where(t == 0, nlb, nlb - t))

    def pair(arr, shape, transposed=False):
        specs = []
        for d in range(2):
            if transposed:
                imap = lambda h, t, d=d: (d, h, 0, rb[d](t))
            else:
                imap = lambda h, t, d=d: (d, h, rb[d](t), 0)
            specs.append(pl.BlockSpec((1, SCAN_HEADS) + shape, imap))
        return specs, [arr, arr]

    in_specs, args = [], []
    for arr, shape, tr in ((u, (SCAN_ROWS, HEAD_DIM), False), (w, (SCAN_ROWS, HEAD_DIM), False),
                           (qd, (SCAN_ROWS, HEAD_DIM), False), (kdt, (HEAD_DIM, SCAN_ROWS), True),
                           (qk, (SCAN_ROWS, HEAD_DIM), False), (tot_row, (1, SCAN_ROWS), True)):
        sp, ar = pair(arr, shape, tr)
        in_specs += sp
        args += ar
    out_specs = [pl.BlockSpec((SCAN_ROWS, SCAN_HEADS * HEAD_DIM), lambda h, t, d=d: (rb[d](t), h)) for d in range(2)]
    return pl.pallas_call(
        _dn_scan_kernel,
        grid=(N_HEADS // SCAN_HEADS, nlb + 1),
        in_specs=in_specs,
        out_specs=out_specs,
        out_shape=[jax.ShapeDtypeStruct((r, DN_DIM), F32)] * 2,
        scratch_shapes=[pltpu.VMEM((2, SCAN_HEADS, HEAD_DIM, HEAD_DIM), F32)],
        compiler_params=_cparams(("arbitrary", "arbitrary")),
        name="dn_scan",
    )(*args)


def _dn_out_kernel(of_ref, ob_ref, gate_ref, ng_ref, y_ref):
    for h in range(N_HEADS):
        cs = slice(h * HEAD_DIM, (h + 1) * HEAD_DIM)
        o = of_ref[:, cs] + ob_ref[:, cs]
        o = o * lax.rsqrt(jnp.mean(o * o, axis=-1, keepdims=True) + 1e-6) * ng_ref[...]
        y_ref[:, cs] = (o * _silu(gate_ref[:, cs])).astype(y_ref.dtype)


def _dn_out(o_f, o_b, gate, norm_g):
    r, n = o_f.shape
    row = pl.BlockSpec((ELT_ROWS, n), lambda i: (i, 0))
    return pl.pallas_call(
        _dn_out_kernel,
        grid=(r // ELT_ROWS,),
        in_specs=[row, row, row, pl.BlockSpec((1, HEAD_DIM), lambda i: (0, 0))],
        out_specs=row,
        out_shape=jax.ShapeDtypeStruct((r, n), BF16),
        compiler_params=_cparams(("arbitrary",)),
        name="dn_out",
    )(o_f, o_b, gate, norm_g.astype(F32).reshape(1, HEAD_DIM))


S5_TG = LANES // S5_GROUP
S5_FOLD = S5_CHUNK * LANES
S5_SW = S5_TG * 2 * S5_STATE
S5_PAIR = 2 * SUBLANES


def _s5_tile_params(lam_re, lam_im, log_step, b_re, b_im, c_re, c_im):
    f = lambda a: a.astype(F32)
    lr, li, br, bi, cr, ci = map(f, (lam_re, lam_im, b_re, b_im, c_re, c_im))
    dt = jnp.exp(f(log_step))[..., None]
    mag, ang = jnp.exp(lr * dt), li * dt
    lbr, lbi = mag * jnp.cos(ang), mag * jnp.sin(ang)
    den = lr * lr + li * li
    zr = ((lbr - 1.0) * lr + lbi * li) / den
    zi = (lbi * lr - (lbr - 1.0) * li) / den
    bbr = zr[..., None] * br - zi[..., None] * bi
    bbi = zr[..., None] * bi + zi[..., None] * br
    g, p = lbr.shape[1], lbr.shape[2]
    nt = g // S5_TG
    tile_rows = lambda a: a.reshape(2, nt, LANES, p)
    per_row = lambda a: tile_rows(jnp.repeat(a, S5_GROUP, axis=1))
    cat = lambda a, b: jnp.concatenate([a, b], axis=-1)
    lrr = cat(per_row(lbr), per_row(lbr))
    lii = cat(-per_row(lbi), per_row(lbi))
    bbc = cat(tile_rows(jnp.swapaxes(bbr, 2, 3)), tile_rows(jnp.swapaxes(bbi, 2, 3)))
    cc = cat(tile_rows(cr), tile_rows(ci))

    def cmul(xr, xi, yr, yi):
        return xr * yr - xi * yi, xr * yi + xi * yr

    ar, ai = lbr, lbi
    for _ in range(int(math.log2(S5_CHUNK))):
        ar, ai = cmul(ar, ai, ar, ai)
    pw = [(ar, ai)]
    for _ in range(SUBLANES - 1):
        pw.append(cmul(*pw[-1], ar, ai))
    plane = lambda x: x.reshape(2, nt, S5_TG * p)
    rows = lambda xr, xi: jnp.stack([plane(xr), plane(xi)], axis=2)
    steps = jnp.stack([jnp.broadcast_to(rows(*pw[n - 1])[:, :, :, None, :], (2, nt, 2, SUBLANES, S5_TG * p))
                       for n in (1, 2, 4)], axis=2)
    fwd = jnp.stack([rows(*pw[j])[0] for j in range(SUBLANES)], axis=2)
    bwd = jnp.stack([rows(*pw[SUBLANES - 1 - j])[1] for j in range(SUBLANES)], axis=2)
    consts = jnp.concatenate([steps, jnp.stack([fwd, bwd])[:, :, None]], axis=2)
    return lrr, lii, bbc, cc, consts.transpose(1, 0, 2, 3, 4, 5)


def _cmul_lanes(x, rr, ii):
    return x * rr + pltpu.roll(x, S5_STATE, 1) * ii


def _s5_ops_kernel(lrr_ref, lii_ref, bbc_ref, cc_ref, *out_refs, toeplitz):
    row = lax.broadcasted_iota(jnp.int32, (LANES, LANES), 0)
    lane = lax.broadcasted_iota(jnp.int32, (LANES, LANES), 1)
    same_group = (row // S5_GROUP) == (lane // S5_GROUP)
    conj = jnp.where(lane < S5_STATE, 1.0, -1.0)
    c = S5_CHUNK
    lag = []
    for d in range(2):
        rr, ii = lrr_ref[d, 0], lii_ref[d, 0]
        xs, zs = [bbc_ref[d, 0]], [cc_ref[d, 0]]
        for _ in range(c):
            xs.append(_cmul_lanes(xs[-1], rr, ii))
            zs.append(_cmul_lanes(zs[-1], rr, ii))
        if toeplitz:
            zc = zs[0] * conj
            z1, z2 = _split2(zc)
            ks = []
            for l in range(c):
                x1, x2 = _split2(xs[l])
                k = (lax.dot_general(x1, z1, _NT, preferred_element_type=F32)
                     + lax.dot_general(x1, z2, _NT, preferred_element_type=F32)
                     + lax.dot_general(x2, z1, _NT, preferred_element_type=F32))
                ks.append(jnp.where(same_group, k, 0.0))
            lag.append(ks)
        else:
            bc_ref, cct_ref = out_refs
            for i in range(c):
                xin = xs[c - 1 - i] if d == 0 else xs[i]
                zout = (zs[i + 1] if d == 0 else zs[c - i]) * conj
                rs = slice(i * LANES, (i + 1) * LANES)
                low = lane < S5_STATE
                for src, dst in ((xin, bc_ref), (zout, cct_ref)):
                    swapped = pltpu.roll(src, S5_STATE, 1)
                    for t in range(S5_TG // 2):
                        even = (row // S5_GROUP) == 2 * t
                        odd = (row // S5_GROUP) == 2 * t + 1
                        re = jnp.where(low, jnp.where(even, src, 0.0), jnp.where(odd, swapped, 0.0))
                        im = jnp.where(low, jnp.where(even, swapped, 0.0), jnp.where(odd, src, 0.0))
                        dst[d, 0, rs, t * LANES:(t + 1) * LANES] = re.astype(BF16)
                        dst[d, 0, rs, S5_SW // 2 + t * LANES:S5_SW // 2 + (t + 1) * LANES] = im.astype(BF16)
    if toeplitz:
        (m_ref,) = out_refs
        fwd = [k.astype(BF16) for k in lag[0]]
        bwd = [k.astype(BF16) for k in lag[1]]
        diag = (lag[0][0] + lag[1][0]).astype(BF16)
        for i in range(c):
            for j in range(c):
                blk = diag if i == j else (fwd[j - i] if j > i else bwd[i - j])
                m_ref[0, i * LANES:(i + 1) * LANES, j * LANES:(j + 1) * LANES] = blk


def _s5_ops(lrr, lii, bbc, cc):
    nt = lrr.shape[1]
    par = pl.BlockSpec((2, 1, LANES, LANES), lambda q: (0, q, 0, 0))
    common = dict(grid=(nt,), in_specs=[par] * 4, compiler_params=_cparams(("arbitrary",)))
    m = pl.pallas_call(
        functools.partial(_s5_ops_kernel, toeplitz=True),
        out_specs=pl.BlockSpec((1, S5_FOLD, S5_FOLD), lambda q: (q, 0, 0)),
        out_shape=jax.ShapeDtypeStruct((nt, S5_FOLD, S5_FOLD), BF16),
        name="s5_ops_toeplitz", **common)(lrr, lii, bbc, cc)
    st = pl.BlockSpec((2, 1, S5_FOLD, S5_SW), lambda q: (0, q, 0, 0))
    bc, cct = pl.pallas_call(
        functools.partial(_s5_ops_kernel, toeplitz=False),
        out_specs=[st, st],
        out_shape=[jax.ShapeDtypeStruct((2, nt, S5_FOLD, S5_SW), BF16)] * 2,
        name="s5_ops_state", **common)(lrr, lii, bbc, cc)
    return m, bc, cct


def _s5_state_kernel(ut_ref, bc_ref, k_ref, e_ref, uf_ref, vf_ref, vb_ref, *, n_lat_c, n_ctx_c):
    n_all = n_lat_c + n_ctx_c
    for i in range(S5_CHUNK):
        uf_ref[0, :, i * LANES:(i + 1) * LANES] = ut_ref[0, pl.ds(i, n_all, stride=S5_CHUNK), :].astype(BF16)
    u = uf_ref[0]
    vf = _dot(u, bc_ref[0, 0])
    vf_ref[pl.ds(0, n_ctx_c), :] = vf[n_lat_c:]
    vf_ref[pl.ds(n_ctx_c, n_lat_c), :] = vf[:n_lat_c]
    vb_ref[...] = _dot(u, bc_ref[1, 0])
    sub = lax.broadcasted_iota(jnp.int32, (SUBLANES, LANES), 0)

    half = S5_SW // 2
    lanes = [(d, slice(g * LANES, (g + 1) * LANES), slice(half + g * LANES, half + (g + 1) * LANES))
             for d in range(2) for g in range(half // LANES)]

    def shift(x, d, sh):
        if d == 0:
            return jnp.where(sub >= sh, pltpu.roll(x, sh, 0), 0.0)
        return jnp.where(sub < SUBLANES - sh, pltpu.roll(x, SUBLANES - sh, 0), 0.0)

    def axpy(x, s, d, n, cs):
        ar, ai = k_ref[0, d, n, 0, :, cs], k_ref[0, d, n, 1, :, cs]
        return x[0] + (s[0] * ar - s[1] * ai), x[1] + (s[1] * ar + s[0] * ai)

    def tile_scans(xs, carry_rows):
        for n, sh in enumerate((1, 2, 4)):
            shifted = [(shift(x[0], d, sh), shift(x[1], d, sh)) for x, (d, _, _) in zip(xs, lanes)]
            xs = [axpy(x, s, d, n, cr) for x, s, (d, cr, _) in zip(xs, shifted, lanes)]
        cbs = [tuple(jnp.broadcast_to(c, (SUBLANES, LANES)) for c in cc) for cc in carry_rows]
        xs = [axpy(x, cb, d, 3, cr) for x, cb, (d, cr, _) in zip(xs, cbs, lanes)]
        out = []
        for x, cb, (d, _, _) in zip(xs, cbs, lanes):
            if d == 0:
                excl = tuple(jnp.where(sub == 0, c, pltpu.roll(v, 1, 0)) for v, c in zip(x, cb))
                out.append((excl, tuple(v[SUBLANES - 1:SUBLANES, :] for v in x)))
            else:
                excl = tuple(jnp.where(sub == SUBLANES - 1, c, pltpu.roll(v, SUBLANES - 1, 0)) for v, c in zip(x, cb))
                out.append((excl, tuple(v[0:1, :] for v in x)))
        return out

    n_iter = n_all // S5_PAIR
    src = (vf_ref, vb_ref)

    def body(t, carries):
        base = (pl.multiple_of(t * S5_PAIR, S5_PAIR), pl.multiple_of((n_iter - 1 - t) * S5_PAIR, S5_PAIR))
        load = lambda off: [tuple(src[d][pl.ds(base[d] + off(d) * SUBLANES, SUBLANES), cs] for cs in (cr, ci))
                            for d, cr, ci in lanes]
        first = tile_scans(load(lambda d: d), carries)
        second = tile_scans(load(lambda d: 1 - d), [c for _, c in first])
        for (d, cr, ci), (e1, _), (e2, _) in zip(lanes, first, second):
            lo, hi = (e1, e2) if d == 0 else (e2, e1)
            for part, cs in enumerate((cr, ci)):
                e_ref[d, 0, pl.ds(base[d], S5_PAIR), cs] = jnp.concatenate([lo[part], hi[part]], axis=0).astype(BF16)
        return tuple(c for _, c in second)

    zero = jnp.zeros((1, LANES), F32)
    lax.fori_loop(0, n_iter, body, tuple((zero, zero) for _ in lanes))


def _s5_out_kernel(uf_ref, m_ref, cct_ref, e_ref, y_ref, *, n_lat_c, n_ctx_c):
    y = _dot(uf_ref[0, pl.ds(0, n_lat_c), :], m_ref[0])
    y = y + lax.dot_general(e_ref[0, 0, pl.ds(n_ctx_c, n_lat_c), :], cct_ref[0, 0], _NT, preferred_element_type=F32)
    y = y + lax.dot_general(e_ref[1, 0, pl.ds(0, n_lat_c), :], cct_ref[1, 0], _NT, preferred_element_type=F32)
    first = pl.program_id(1) * (S5_SW // LANES)
    for ii in range(S5_SW // LANES):
        y_ref[0, pl.ds(first + ii, n_lat_c, stride=S5_CHUNK), :] = y[:, ii * LANES:(ii + 1) * LANES]


def _s5_scan(ut, m, bc, cct, consts, *, n_lat):
    nt, r, _ = ut.shape
    n_all = r // S5_CHUNK
    n_lat_c = n_lat // S5_CHUNK
    n_ctx_c = n_all - n_lat_c
    assert n_lat_c % S5_PAIR == 0 and n_ctx_c % S5_PAIR == 0
    dims = dict(n_lat_c=n_lat_c, n_ctx_c=n_ctx_c)
    e, uf = pl.pallas_call(
        functools.partial(_s5_state_kernel, **dims),
        grid=(nt,),
        in_specs=[
            pl.BlockSpec((1, r, LANES), lambda q: (q, 0, 0)),
            pl.BlockSpec((2, 1, S5_FOLD, S5_SW), lambda q: (0, q, 0, 0)),
            pl.BlockSpec((1, 2, 4, 2, SUBLANES, S5_SW // 2), lambda q: (q, 0, 0, 0, 0, 0)),
        ],
        out_specs=[pl.BlockSpec((2, 1, n_all, S5_SW), lambda q: (0, q, 0, 0)),
                   pl.BlockSpec((1, n_all, S5_FOLD), lambda q: (q, 0, 0))],
        out_shape=[jax.ShapeDtypeStruct((2, nt, n_all, S5_SW), BF16),
                   jax.ShapeDtypeStruct((nt, n_all, S5_FOLD), BF16)],
        scratch_shapes=[pltpu.VMEM((n_all, S5_SW), F32), pltpu.VMEM((n_all, S5_SW), F32)],
        compiler_params=_cparams(("arbitrary",)),
        name="s5_state",
    )(ut, bc, consts)
    halves = S5_FOLD // S5_SW
    return pl.pallas_call(
        functools.partial(_s5_out_kernel, **dims),
        grid=(nt, halves),
        in_specs=[
            pl.BlockSpec((1, n_all, S5_FOLD), lambda q, j: (q, 0, 0)),
            pl.BlockSpec((1, S5_FOLD, S5_SW), lambda q, j: (q, 0, j)),
            pl.BlockSpec((2, 1, S5_SW, S5_SW), lambda q, j: (0, q, j, 0)),
            pl.BlockSpec((2, 1, n_all, S5_SW), lambda q, j: (0, q, 0, 0)),
        ],
        out_specs=pl.BlockSpec((1, n_lat, LANES), lambda q, j: (q, 0, 0)),
        out_shape=jax.ShapeDtypeStruct((nt, n_lat, LANES), F32),
        compiler_params=_cparams(("arbitrary", "arbitrary")),
        name="s5_out",
    )(uf, m, cct, e)


def _gelu_skip_kernel(y_ref, u_ref, d_ref, o_ref):
    for q in range(y_ref.shape[0]):
        cs = slice(q * LANES, (q + 1) * LANES)
        z = y_ref[q] + d_ref[:, cs] * u_ref[q]
        inner = math.sqrt(2.0 / math.pi) * (z + 0.044715 * (z * z * z))
        o_ref[:, cs] = (0.5 * z * (1.0 + jnp.tanh(inner))).astype(o_ref.dtype)


def _gelu_skip(y, u, d_skip):
    nt, r, _ = y.shape
    d = nt * LANES
    tile = pl.BlockSpec((nt, ELT_ROWS, LANES), lambda i: (0, i, 0))
    row = pl.BlockSpec((ELT_ROWS, d), lambda i: (i, 0))
    return pl.pallas_call(
        _gelu_skip_kernel,
        grid=(r // ELT_ROWS,),
        in_specs=[tile, tile, pl.BlockSpec((1, d), lambda i: (0, 0))],
        out_specs=row,
        out_shape=jax.ShapeDtypeStruct((r, d), BF16),
        compiler_params=_cparams(("arbitrary",)),
        name="gelu_skip",
    )(y, u, d_skip.astype(F32).reshape(1, d))


def _mixer_na_gdn(u, w_in, w_out, rpb, conv_w, a_log, dt_bias, norm_g, *, n_lat):
    r = u.shape[0]
    n_main = 3 * NA_DIM + 4 * DN_DIM
    qkv_na = _matmul(u, w_in, col0=0, ncols=3 * NA_DIM, tn=256, out_dtype=BF16)
    qkv_dn = _matmul(u, w_in, col0=3 * NA_DIM, ncols=3 * DN_DIM, tn=256, out_dtype=F32)
    gate = _matmul(u, w_in, col0=3 * NA_DIM + 3 * DN_DIM, ncols=DN_DIM, tn=256, out_dtype=F32)
    w_ab = jnp.pad(w_in[:, n_main:], ((0, 0), (0, LANES - 4 * N_HEADS)))
    ab = _matmul(u, w_ab, col0=0, ncols=LANES, tn=LANES, out_dtype=F32)

    o_na = _na_attention(qkv_na, _na_bias_table(rpb), n_lat=n_lat)

    cos, sin = _rope_tables(n_lat)
    qk = _dn_conv(qkv_dn, conv_w.astype(F32), cos, sin, n_lat=n_lat, col0=0, ncols=2 * DN_DIM, qk=True)
    v = _dn_conv(qkv_dn, conv_w.astype(F32), cos, sin, n_lat=n_lat, col0=2 * DN_DIM, ncols=DN_DIM, qk=False)
    gb = _dn_gates(ab, a_log, dt_bias)
    gb_t = gb.T.reshape(4, 2, N_HEADS, 1, r)
    u_c, w_c, qd, kdt, qkm = _dn_prep(qk, v, gb_t[2], gb_t[3], gb_t[1])
    o_f, o_b = _dn_scan(u_c, w_c, qd, kdt, qkm, gb_t[3], n_lat=n_lat)
    o_dn = _dn_out(o_f, o_b, gate, norm_g)
    return _matmul_cat(o_na, o_dn, w_out, tn=512, out_dtype=BF16)


def _mixer_s5(ut, lam_re, lam_im, log_step, b_re, b_im, c_re, c_im, d_skip, w_out, w_gate, *, n_lat):
    d = ut.shape[0] * LANES
    lrr, lii, bbc, cc, consts = _s5_tile_params(lam_re, lam_im, log_step, b_re, b_im, c_re, c_im)
    m, bc, cct = _s5_ops(lrr, lii, bbc, cc)
    y = _s5_scan(ut, m, bc, cct, consts, n_lat=n_lat)
    gz = _gelu_skip(y, ut, d_skip)
    return _matmul_pair(gz, w_out, w_gate, col1=0, col2=0, ncols=d, tn=256, gate_second=True, out_dtype=BF16)


def kernel(x, c, ctx, c_ctx, w_mod, b_mod, ln_g, ln_b, ffn_w_in, ffn_w_out, ab_w_in, ab_w_out, na_rpb, dn_conv_w, dn_a_log, dn_dt_bias, dn_norm_g, s5_lam_re, s5_lam_im, s5_log_step, s5_b_re, s5_b_im, s5_c_re, s5_c_im, s5_d, s5_w_out, s5_w_gate):
    depth, d = w_mod.shape[0], x.shape[2]
    assert depth == 2 and x.shape[0] == 1, "layer 0 is the NA/DeltaNet layer, layer 1 the final S5 layer"
    n_lat = x.shape[1]
    alpha = (2.0 * depth) ** 0.25
    h = jnp.concatenate([x[0], ctx[0]], axis=0)
    mods = _modulation(c, c_ctx, w_mod, b_mod)
    lng = ln_g.astype(F32).reshape(depth * 3, 1, d)
    lnb = ln_b.astype(F32).reshape(depth * 3, 1, d)
    res_ln = functools.partial(_res_ln, mods=mods, ln_g=lng, ln_b=lnb, alpha=alpha, n_lat=n_lat)
    w_out_bf = ffn_w_out.astype(BF16)
    ffn = lambda uu, l, s: _ffn(uu, ffn_w_in, w_out_bf, (l, s))

    u = _modulate(h, mods, layer=0, sub=0, n_lat=n_lat)
    h, u = res_ln(h, ffn(u, 0, 0), layer=0, sub=0, weight=MACARON_WEIGHT, next_mod=(0, 1))
    y = _mixer_na_gdn(u, ab_w_in[0], ab_w_out[0], na_rpb[0], dn_conv_w[0], dn_a_log[0], dn_dt_bias[0], dn_norm_g[0],
                      n_lat=n_lat)
    h, u = res_ln(h, y, layer=0, sub=1, weight=1.0, next_mod=(0, 2))
    h, u = res_ln(h, ffn(u, 0, 1), layer=0, sub=2, weight=MACARON_WEIGHT, next_mod=(1, 0))

    h, ut = res_ln(h, ffn(u, 1, 0), layer=1, sub=0, weight=MACARON_WEIGHT, next_mod=(1, 1), u_dtypes=(F32,), u_tiles=True)
    y = _mixer_s5(ut, s5_lam_re[0], s5_lam_im[0], s5_log_step[0], s5_b_re[0], s5_b_im[0], s5_c_re[0], s5_c_im[0],
                  s5_d[0], s5_w_out[0], s5_w_gate[0], n_lat=n_lat)
    h, u = res_ln(h, y, layer=1, sub=1, weight=1.0, next_mod=(1, 2))
    (out,) = res_ln(h, ffn(u, 1, 1), layer=1, sub=2, weight=MACARON_WEIGHT, next_mod=None)
    return out[None]
```

```python
import functools
import math

import numpy as np
import jax
import jax.numpy as jnp
from jax import lax
from jax.experimental import pallas as pl
from jax.experimental.pallas import tpu as pltpu

GRID_W = 64
HEAD_DIM = 128
N_HEADS = 16
NA_DIM = N_HEADS * HEAD_DIM
DN_DIM = N_HEADS * HEAD_DIM
NA_WIN_ROWS = 8
NA_WIN_COLS = 16
DN_CONV = 5
DN_CHUNK = 64
ROPE_BASE = 10000.0
S5_GROUP = 16
S5_STATE = 64
S5_CHUNK = 16
N_MOD = 9
LN_EPS = 1e-5
MACARON_WEIGHT = 0.5

V7X_VMEM_BYTES = 64 * 1024 * 1024
VMEM_LIMIT = V7X_VMEM_BYTES - 8 * 1024 * 1024
LANES = 128
SUBLANES = 8

F32 = jnp.float32
BF16 = jnp.bfloat16


def _cparams(sem):
    return pltpu.CompilerParams(dimension_semantics=sem, vmem_limit_bytes=VMEM_LIMIT)


def _silu(x):
    return x * (1.0 / (1.0 + jnp.exp(-x)))


def _sigmoid(x):
    return 1.0 / (1.0 + jnp.exp(-x))


def _row_tile(rows, candidates=(768, 512, 256)):
    for t in candidates:
        if rows % t == 0:
            return t
    raise ValueError(f"unsupported row count {rows}")


F32_WEIGHT_ROW_TILES = (1408, 1024, 768, 512, 256)


def _weight_row_tile(rows, w):
    return _row_tile(rows, F32_WEIGHT_ROW_TILES if w.dtype == F32 else (768, 512, 256))


def _weight_spec(lead, k, tn, col_block0, n_grid):
    if n_grid == 2:
        imap = lambda i, j: lead + (0, j + col_block0)
    else:
        imap = lambda i, j, kk: lead + (kk, j + col_block0)
    return pl.BlockSpec((None,) * len(lead) + (k, tn), imap)


def _mod_kernel(c_ref, w_ref, b_ref, o_ref):
    s = _silu(c_ref[...]).astype(BF16)
    w = w_ref[0].astype(BF16)
    o_ref[0] = jnp.dot(s, w, preferred_element_type=F32) + b_ref[0]


def _modulation(c, c_ctx, w_mod, b_mod):
    depth, d, n = w_mod.shape
    tn = 512
    cc = jnp.zeros((SUBLANES, d), F32).at[0].set(c[0]).at[1].set(c_ctx)
    out = pl.pallas_call(
        _mod_kernel,
        grid=(depth, n // tn),
        in_specs=[
            pl.BlockSpec((SUBLANES, d), lambda l, j: (0, 0)),
            pl.BlockSpec((1, d, tn), lambda l, j: (l, 0, j)),
            pl.BlockSpec((1, 1, tn), lambda l, j: (l, 0, j)),
        ],
        out_specs=pl.BlockSpec((1, SUBLANES, tn), lambda l, j: (l, 0, j)),
        out_shape=jax.ShapeDtypeStruct((depth, SUBLANES, n), F32),
        compiler_params=_cparams(("arbitrary", "arbitrary")),
        name="modulation",
    )(cc, w_mod, b_mod.reshape(depth, 1, n))
    return out[:, :2].reshape(depth * 2 * N_MOD, 1, d)


def _mm_kernel(a_ref, w_ref, o_ref):
    o_ref[...] = jnp.dot(a_ref[...], w_ref[...].astype(BF16), preferred_element_type=F32).astype(o_ref.dtype)


def _matmul(a, w, *, lead=(), col0, ncols, tn, out_dtype):
    r, k = a.shape
    tm = _weight_row_tile(r, w)
    assert col0 % tn == 0 and ncols % tn == 0 and w.shape[len(lead)] == k
    return pl.pallas_call(
        _mm_kernel,
        grid=(r // tm, ncols // tn),
        in_specs=[pl.BlockSpec((tm, k), lambda i, j: (i, 0)), _weight_spec(lead, k, tn, col0 // tn, 2)],
        out_specs=pl.BlockSpec((tm, tn), lambda i, j: (i, j)),
        out_shape=jax.ShapeDtypeStruct((r, ncols), out_dtype),
        compiler_params=_cparams(("arbitrary", "arbitrary")),
        name="matmul",
    )(a, w)


def _mm_pair_kernel(a_ref, w1_ref, w2_ref, o_ref, *, gate_second):
    a = a_ref[...]
    p1 = jnp.dot(a, w1_ref[...].astype(BF16), preferred_element_type=F32)
    p2 = jnp.dot(a, w2_ref[...].astype(BF16), preferred_element_type=F32)
    if gate_second:
        o = p1 * _sigmoid(p2)
    else:
        o = _silu(p1) * p2
    o_ref[...] = o.astype(o_ref.dtype)


def _matmul_pair(a, w1, w2, *, lead=(), col1, col2, ncols, tn, gate_second, out_dtype):
    r, k = a.shape
    tm = _weight_row_tile(r, w1)
    assert col1 % tn == 0 and col2 % tn == 0 and ncols % tn == 0
    return pl.pallas_call(
        functools.partial(_mm_pair_kernel, gate_second=gate_second),
        grid=(r // tm, ncols // tn),
        in_specs=[pl.BlockSpec((tm, k), lambda i, j: (i, 0)),
                  _weight_spec(lead, k, tn, col1 // tn, 2), _weight_spec(lead, k, tn, col2 // tn, 2)],
        out_specs=pl.BlockSpec((tm, tn), lambda i, j: (i, j)),
        out_shape=jax.ShapeDtypeStruct((r, ncols), out_dtype),
        compiler_params=_cparams(("arbitrary", "arbitrary")),
        name="matmul_pair",
    )(a, w1, w2)


def _mm_cat_kernel(a1_ref, a2_ref, w_ref, o_ref, acc_ref):
    k = pl.program_id(2)

    @pl.when(k == 0)
    def _():
        acc_ref[...] = jnp.dot(a1_ref[...], w_ref[...].astype(BF16), preferred_element_type=F32)

    @pl.when(k == 1)
    def _():
        o_ref[...] = (acc_ref[...]
                      + jnp.dot(a2_ref[...], w_ref[...].astype(BF16), preferred_element_type=F32)).astype(o_ref.dtype)


def _matmul_cat(a1, a2, w, *, tn, out_dtype):
    r, k1 = a1.shape
    assert a2.shape == (r, k1) and w.shape[0] == 2 * k1
    n = w.shape[1]
    tm = _weight_row_tile(r, w)
    return pl.pallas_call(
        _mm_cat_kernel,
        grid=(r // tm, n // tn, 2),
        in_specs=[
            pl.BlockSpec((tm, k1), lambda i, j, k: (i, 0)),
            pl.BlockSpec((tm, k1), lambda i, j, k: (i, 0)),
            _weight_spec((), k1, tn, 0, 3),
        ],
        out_specs=pl.BlockSpec((tm, tn), lambda i, j, k: (i, j)),
        out_shape=jax.ShapeDtypeStruct((r, n), out_dtype),
        scratch_shapes=[pltpu.VMEM((tm, tn), F32)],
        compiler_params=_cparams(("arbitrary", "arbitrary", "arbitrary")),
        name="matmul_cat",
    )(a1, a2, w)


ELT_ROWS = 256


def _modulate_kernel(z_ref, shift_ref, scale_ref, u_ref):
    u_ref[...] = (z_ref[...] * (1.0 + scale_ref[0]) + shift_ref[0]).astype(u_ref.dtype)


def _mod_row(layer, r, n_lat_tiles):
    return lambda i: ((layer * 2 + jnp.where(i >= n_lat_tiles, 1, 0)) * N_MOD + r, 0, 0)


def _modulate(z, mods, *, layer, sub, n_lat):
    r, d = z.shape
    nlt = n_lat // ELT_ROWS
    vec = lambda row: pl.BlockSpec((1, 1, d), _mod_row(layer, row, nlt))
    return pl.pallas_call(
        _modulate_kernel,
        grid=(r // ELT_ROWS,),
        in_specs=[pl.BlockSpec((ELT_ROWS, d), lambda i: (i, 0)), vec(3 * sub), vec(3 * sub + 1)],
        out_specs=pl.BlockSpec((ELT_ROWS, d), lambda i: (i, 0)),
        out_shape=jax.ShapeDtypeStruct((r, d), BF16),
        compiler_params=_cparams(("arbitrary",)),
        name="modulate",
    )(z, mods, mods)


def _res_ln_kernel(*refs, alpha, weight, n_u):
    z_ref, y_ref, gate_ref, g_ref, b_ref = refs[:5]
    t = alpha * z_ref[...] + (weight * gate_ref[0]) * y_ref[...].astype(F32)
    mu = jnp.mean(t, axis=-1, keepdims=True)
    tc = t - mu
    var = jnp.mean(tc * tc, axis=-1, keepdims=True)
    x = tc * lax.rsqrt(var + LN_EPS) * g_ref[0] + b_ref[0]
    if n_u == 0:
        refs[5][...] = x
        return
    shift_ref, scale_ref, x_ref = refs[5], refs[6], refs[7]
    x_ref[...] = x
    u = x * (1.0 + scale_ref[0]) + shift_ref[0]
    for u_ref in refs[8:]:
        if len(u_ref.shape) == 3:
            for q in range(u_ref.shape[0]):
                u_ref[q] = u[:, q * LANES:(q + 1) * LANES].astype(u_ref.dtype)
        else:
            u_ref[...] = u.astype(u_ref.dtype)


def _res_ln(z, y, *, mods, ln_g, ln_b, layer, sub, weight, alpha, n_lat, next_mod, u_dtypes=(BF16,), u_tiles=False):
    r, d = y.shape
    nlt = n_lat // ELT_ROWS
    vec = lambda lyr, row: pl.BlockSpec((1, 1, d), _mod_row(lyr, row, nlt))
    ln_idx = layer * 3 + sub
    ln_spec = pl.BlockSpec((1, 1, d), lambda i: (ln_idx, 0, 0))
    row_spec = pl.BlockSpec((ELT_ROWS, d), lambda i: (i, 0))
    in_specs = [row_spec, row_spec, vec(layer, 3 * sub + 2), ln_spec, ln_spec]
    args = [z, y, mods, ln_g, ln_b]
    out_specs = [row_spec]
    out_shape = [jax.ShapeDtypeStruct((r, d), F32)]
    n_u = 0
    if next_mod is not None:
        nl, ns = next_mod
        in_specs += [vec(nl, 3 * ns), vec(nl, 3 * ns + 1)]
        args += [mods, mods]
        n_u = len(u_dtypes)
        if u_tiles:
            out_specs += [pl.BlockSpec((d // LANES, ELT_ROWS, LANES), lambda i: (0, i, 0))] * n_u
            out_shape += [jax.ShapeDtypeStruct((d // LANES, r, LANES), dt) for dt in u_dtypes]
        else:
            out_specs += [row_spec] * n_u
            out_shape += [jax.ShapeDtypeStruct((r, d), dt) for dt in u_dtypes]
    res = pl.pallas_call(
        functools.partial(_res_ln_kernel, alpha=alpha, weight=weight, n_u=n_u),
        grid=(r // ELT_ROWS,),
        in_specs=in_specs,
        out_specs=out_specs,
        out_shape=out_shape,
        compiler_params=_cparams(("arbitrary",)),
        name="res_ln",
    )(*args)
    return res


def _ffn(u, w_in, w_out, lead):
    d_ff, d = w_out.shape[-2:]
    hmid = _matmul_pair(u, w_in, w_in, lead=lead, col1=0, col2=d_ff, ncols=d_ff, tn=256, gate_second=False,
                        out_dtype=BF16)
    return _matmul(hmid, w_out, lead=lead, col0=0, ncols=d, tn=512, out_dtype=BF16)


NEG_BIG = -1e30
NA_UNROLL = 16
_NT =(((1,), (1,)), ((), ()))


def _na_bias_table(rpb):
    kr, kw, w = NA_WIN_ROWS, NA_WIN_COLS, GRID_W
    pat = np.arange(kr)[:, None]
    i = np.arange(kr)[None, :]
    ridx = i - pat + (NA_WIN_ROWS - 1)
    col = np.arange(w)
    cs = np.clip(col - kw // 2, 0, w - kw)
    c2 = np.arange(w)[None, :]
    valid = (c2 >= cs[:, None]) & (c2 < cs[:, None] + kw)
    cidx = np.clip(c2 - col[:, None] + (NA_WIN_COLS - 1), 0, 2 * NA_WIN_COLS - 2)
    row_sel = (ridx[:, :, None] == np.arange(2 * kr - 1)).astype(np.float32)
    col_sel = ((cidx[:, :, None] == np.arange(2 * kw - 1)) & valid[:, :, None]).astype(np.float32)
    by_row = jnp.einsum('pia,hab->hpib', jnp.asarray(row_sel), rpb.astype(F32), precision=lax.Precision.HIGHEST)
    tbl = jnp.einsum('hpib,cdb->hpcid', by_row, jnp.asarray(col_sel), precision=lax.Precision.HIGHEST)
    tbl = tbl + jnp.asarray(np.where(valid, 0.0, NEG_BIG).astype(np.float32))[None, None, :, None, :]
    return tbl.reshape(rpb.shape[0], kr, w, kr * w)


def _na_kernel(q_ref, k_ref, v_ref, bias_ref, o_ref, *, n_lat, n_ctx):
    rows = n_lat // GRID_W
    kr = NA_WIN_ROWS
    scale = HEAD_DIM ** -0.5
    kc = k_ref[pl.ds(n_lat, n_ctx), :]
    vc = v_ref[pl.ds(n_lat, n_ctx), :]

    def attend(s_parts, v_parts):
        m = functools.reduce(jnp.maximum, [jnp.max(s, axis=-1, keepdims=True) for s in s_parts])
        ps = [jnp.exp(s - m) for s in s_parts]
        den = functools.reduce(lambda a, b: a + b, [jnp.sum(p, axis=-1, keepdims=True) for p in ps])
        num = functools.reduce(lambda a, b: a + b,
                               [jnp.dot(p.astype(BF16), v, preferred_element_type=F32) for p, v in zip(ps, v_parts)])
        return num / den

    def body(it, carry):
        rs = [it * NA_UNROLL + n for n in range(NA_UNROLL)]
        r0s = [jnp.clip(r - kr // 2, 0, rows - kr) for r in rs]
        qrows = [pl.ds(pl.multiple_of(r * GRID_W, GRID_W), GRID_W) for r in rs]
        krows = [pl.ds(pl.multiple_of(r0 * GRID_W, GRID_W), kr * GRID_W) for r0 in r0s]
        qs = [q_ref[qr, :] for qr in qrows]
        s_loc = [lax.dot_general(q, k_ref[kr_, :], _NT, preferred_element_type=F32) * scale + bias_ref[0, r - r0]
                 for q, kr_, r, r0 in zip(qs, krows, rs, r0s)]
        s_ctx = [lax.dot_general(q, kc, _NT, preferred_element_type=F32) * scale for q in qs]
        outs = [attend([sl, sc], [v_ref[kr_, :], vc]) for sl, sc, kr_ in zip(s_loc, s_ctx, krows)]
        for qr, o in zip(qrows, outs):
            o_ref[qr, :] = o.astype(o_ref.dtype)
        return carry

    assert rows % NA_UNROLL == 0
    lax.fori_loop(0, rows // NA_UNROLL, body, 0)
    qc = q_ref[pl.ds(n_lat, n_ctx), :]
    s_cc = lax.dot_general(qc, kc, _NT, preferred_element_type=F32) * scale
    o_ref[pl.ds(n_lat, n_ctx), :] = attend([s_cc], [vc]).astype(o_ref.dtype)


def _na_attention(qkv, bias_tbl, *, n_lat):
    r = qkv.shape[0]
    n_ctx = r - n_lat
    assert n_lat // GRID_W >= NA_WIN_ROWS
    blk = lambda off: pl.BlockSpec((r, HEAD_DIM), lambda h: (0, h + off))
    return pl.pallas_call(
        functools.partial(_na_kernel, n_lat=n_lat, n_ctx=n_ctx),
        grid=(N_HEADS,),
        in_specs=[blk(0), blk(N_HEADS), blk(2 * N_HEADS),
                  pl.BlockSpec((1,) + bias_tbl.shape[1:], lambda h: (h, 0, 0, 0))],
        out_specs=pl.BlockSpec((r, HEAD_DIM), lambda h: (0, h)),
        out_shape=jax.ShapeDtypeStruct((r, NA_DIM), BF16),
        compiler_params=_cparams(("arbitrary",)),
        name="na_attention",
    )(qkv, qkv, qkv, bias_tbl)


CONV_ROWS = 256
PAD = SUBLANES


def _rope_tables(n_lat):
    t = np.arange(n_lat)
    n_freq = HEAD_DIM // 4
    inv = ROPE_BASE ** (-np.arange(n_freq, dtype=np.float32) / n_freq)
    ang = np.concatenate([(t // GRID_W).astype(np.float32)[:, None] * inv,
                          (t % GRID_W).astype(np.float32)[:, None] * inv], -1).astype(np.float32)
    ang = jnp.asarray(ang)
    cos = jnp.repeat(jnp.cos(ang), 2, axis=-1)
    sin = jnp.repeat(jnp.sin(ang), 2, axis=-1) * jnp.asarray(np.tile(np.array([-1.0, 1.0], np.float32), HEAD_DIM // 2))
    return cos, sin


def _dn_conv_kernel(x_ref, w_ref, cos_ref, sin_ref, o_ref, xp_ref, *, n_lat, n_ctx, qk):
    half = DN_CONV // 2
    zeros = jnp.zeros((PAD, LANES), F32)
    xp_ref[pl.ds(0, PAD), :] = zeros
    xp_ref[pl.ds(PAD, n_lat), :] = x_ref[pl.ds(0, n_lat), :]
    xp_ref[pl.ds(PAD + n_lat, PAD), :] = zeros
    xp_ref[pl.ds(2 * PAD + n_lat, n_ctx), :] = x_ref[pl.ds(n_lat, n_ctx), :]
    xp_ref[pl.ds(2 * PAD + n_lat + n_ctx, PAD), :] = zeros
    w = w_ref[...]
    lane = lax.broadcasted_iota(jnp.int32, (CONV_ROWS, LANES), 1)
    even = (lane % 2) == 0

    def chunk(src0, dst0, rope_row0):
        big = xp_ref[pl.ds(src0 - PAD, CONV_ROWS + 2 * PAD), :]
        acc = jnp.zeros((CONV_ROWS, LANES), F32)
        for j in range(DN_CONV):
            acc = acc + big[PAD + j - half: PAD + j - half + CONV_ROWS, :] * w[j:j + 1, :]
        y = _silu(acc)
        if qk:
            y = y * lax.rsqrt(jnp.sum(y * y, axis=-1, keepdims=True) + 1e-6)
            if rope_row0 is not None:
                swapped = jnp.where(even, pltpu.roll(y, LANES - 1, 1), pltpu.roll(y, 1, 1))
                y = y * cos_ref[pl.ds(rope_row0, CONV_ROWS), :] + swapped * sin_ref[pl.ds(rope_row0, CONV_ROWS), :]
        o_ref[pl.ds(dst0, CONV_ROWS), :] = y

    def lat_body(i, carry):
        r0 = pl.multiple_of(i * CONV_ROWS, CONV_ROWS)
        chunk(r0 + PAD, r0, r0)
        return carry

    lax.fori_loop(0, n_lat // CONV_ROWS, lat_body, 0)
    for i in range(n_ctx // CONV_ROWS):
        chunk(2 * PAD + n_lat + i * CONV_ROWS, n_lat + i * CONV_ROWS, None)


def _dn_conv(qkv, conv_w, cos, sin, *, n_lat, col0, ncols, qk):
    r = qkv.shape[0]
    n_ctx = r - n_lat
    assert n_lat % CONV_ROWS == 0 and n_ctx % CONV_ROWS == 0
    jb = col0 // LANES
    return pl.pallas_call(
        functools.partial(_dn_conv_kernel, n_lat=n_lat, n_ctx=n_ctx, qk=qk),
        grid=(ncols // LANES,),
        in_specs=[
            pl.BlockSpec((r, LANES), lambda j: (0, j + jb)),
            pl.BlockSpec((DN_CONV, LANES), lambda j: (0, j + jb)),
            pl.BlockSpec((n_lat, LANES), lambda j: (0, 0)),
            pl.BlockSpec((n_lat, LANES), lambda j: (0, 0)),
        ],
        out_specs=pl.BlockSpec((r, LANES), lambda j: (0, j)),
        out_shape=jax.ShapeDtypeStruct((r, ncols), F32),
        scratch_shapes=[pltpu.VMEM((r + 3 * PAD, LANES), F32)],
        compiler_params=_cparams(("arbitrary",)),
        name="dn_conv",
    )(qkv, conv_w, cos, sin)


def _dn_gates_kernel(ab_ref, alog_ref, dtb_ref, o_ref):
    ab = ab_ref[...]
    z = ab + dtb_ref[...]
    softplus = jnp.maximum(z, 0.0) + jnp.log(1.0 + jnp.exp(-jnp.abs(z)))
    lane = lax.broadcasted_iota(jnp.int32, ab.shape, 1)
    g = jnp.where(lane < 2 * N_HEADS, -jnp.exp(alog_ref[...]) * softplus, 0.0)
    n = ab.shape[0]
    i = lax.broadcasted_iota(jnp.int32, (n, n), 0)
    j = lax.broadcasted_iota(jnp.int32, (n, n), 1)
    same = (i // DN_CHUNK) == (j // DN_CHUNK)
    one = lambda m: jnp.where(m, 1.0, 0.0).astype(BF16)
    gc = jnp.where(lane < N_HEADS, _dot_exact_left(one(same & (i >= j)), g), _dot_exact_left(one(same & (i <= j)), g))
    tot = _dot_exact_left(one(same), g)
    quarter = LANES // 4
    o_ref[...] = jnp.where(lane < quarter, g,
                           jnp.where(lane < 2 * quarter, _sigmoid(ab),
                                     jnp.where(lane < 3 * quarter, pltpu.roll(gc, 2 * quarter, 1),
                                               pltpu.roll(tot, 3 * quarter, 1))))


def _dn_gates(ab, a_log, dt_bias):
    r = ab.shape[0]
    pad = lambda v: jnp.zeros((1, LANES), F32).at[0, :2 * N_HEADS].set(v.astype(F32).reshape(-1))
    return pl.pallas_call(
        _dn_gates_kernel,
        grid=(r // ELT_ROWS,),
        in_specs=[pl.BlockSpec((ELT_ROWS, LANES), lambda i: (i, 0)),
                  pl.BlockSpec((1, LANES), lambda i: (0, 0)),
                  pl.BlockSpec((1, LANES), lambda i: (0, 0))],
        out_specs=pl.BlockSpec((ELT_ROWS, LANES), lambda i: (i, 0)),
        out_shape=jax.ShapeDtypeStruct((r, LANES), F32),
        compiler_params=_cparams(("arbitrary",)),
        name="dn_gates",
    )(ab, pad(a_log), pad(dt_bias))


BLK = 2 * DN_CHUNK


def _split3(x):
    x1 = x.astype(BF16)
    r1 = x - x1.astype(F32)
    x2 = r1.astype(BF16)
    x3 = (r1 - x2.astype(F32)).astype(BF16)
    return x1, x2, x3


def _dot(a, b):
    return jnp.dot(a, b, preferred_element_type=F32)


def _dot_exact_left(m01, x):
    x1, x2, x3 = _split3(x)
    return _dot(m01, x1) + _dot(m01, x2) + _dot(m01, x3)


def _split2(x):
    x1 = x.astype(BF16)
    return x1, (x - x1.astype(F32)).astype(BF16)


def _dn_chunk_math(units):
    n = BLK
    i = lax.broadcasted_iota(jnp.int32, (n, n), 0)
    j = lax.broadcasted_iota(jnp.int32, (n, n), 1)
    same = (i // DN_CHUNK) == (j // DN_CHUNK)
    masks = {False: (same & (i >= j), same & (i > j)), True: (same & (i <= j), same & (i < j))}
    eye = jnp.where(i == j, 1.0, 0.0)
    each = lambda fn, *lists: [fn(*xs) for xs in zip(*lists)]
    q, k, v, gc_col, gc_row, tot_col, beta_col, rev = map(list, zip(*units))
    incl = [masks[r][0] for r in rev]
    strict = [masks[r][1] for r in rev]
    decay = each(lambda m, c, r: jnp.where(m, jnp.exp(jnp.where(m, c - r, 0.0)), 0.0), incl, gc_col, gc_row)
    kb = each(lambda a, b: a * b, k, beta_col)
    qs = [x * (HEAD_DIM ** -0.5) for x in q]
    kk_qk = each(lambda a, b, c: lax.dot_general(jnp.concatenate([a, b], axis=0).astype(BF16), c.astype(BF16), _NT,
                                                 preferred_element_type=F32), kb, qs, k)
    a = each(lambda m, x, dc: jnp.where(m, x[:n] * dc, 0.0), strict, kk_qk, decay)
    qk = each(lambda m, x, dc: jnp.where(m, x[n:] * dc, 0.0), incl, kk_qk, decay)
    t = [eye - x for x in a]
    ab = [x.astype(BF16) for x in a]
    p = [_dot(x, x) for x in ab]
    levels = int(math.log2(DN_CHUNK)) - 1
    for lvl in range(levels):
        last = lvl == levels - 1
        prod = [_dot((tt if last else jnp.concatenate([tt, pp], axis=0)).astype(BF16), pp.astype(BF16))
                for tt, pp in zip(t, p)]
        t = [tt + pr[:n] for tt, pr in zip(t, prod)]
        if not last:
            p = [pr[n:] for pr in prod]
    eg = [jnp.exp(c) for c in gc_col]
    rhs = each(lambda vv, b, kk, e: jnp.concatenate([(vv * b).astype(BF16), (kk * e).astype(BF16)], axis=1),
               v, beta_col, kb, eg)
    uw = [_dot(tt.astype(BF16), r) for tt, r in zip(t, rhs)]
    kdt = each(lambda kk, tc, c: (kk * jnp.exp(tc - c)).T.astype(BF16), k, tot_col, gc_col)
    return [(x[:, :HEAD_DIM], x[:, HEAD_DIM:].astype(BF16), (qq * e).astype(BF16), kt, m.astype(BF16))
            for x, qq, e, kt, m in zip(uw, qs, eg, kdt, qk)]


PREP_BLOCKS = 2
PREP_HEADS = 4


def _dn_prep_kernel(q_ref, k_ref, v_ref, gc_ref, tot_ref, beta_ref,
                    u_ref, w_ref, qd_ref, kdt_ref, qk_ref):
    where = [(d, hh, pl.ds(b * BLK, BLK)) for b in range(PREP_BLOCKS) for hh in range(PREP_HEADS) for d in range(2)]
    hcols = lambda hh: slice(hh * HEAD_DIM, (hh + 1) * HEAD_DIM)
    col = lambda ref, d, hh, rs: jnp.broadcast_to(ref[d, hh, :, rs], (BLK, BLK)).T
    units = [(q_ref[rs, hcols(hh)], k_ref[rs, hcols(hh)], v_ref[rs, hcols(hh)], col(gc_ref, d, hh, rs),
              gc_ref[d, hh, :, rs], col(tot_ref, d, hh, rs), col(beta_ref, d, hh, rs), d == 1) for d, hh, rs in where]
    for (d, hh, rs), (u, w, qd, kdt, qk) in zip(where, _dn_chunk_math(units)):
        u_ref[d, hh, rs, :] = u
        w_ref[d, hh, rs, :] = w
        qd_ref[d, hh, rs, :] = qd
        kdt_ref[d, hh, :, rs] = kdt
        qk_ref[d, hh, rs, :] = qk


def _dn_prep(qk, v, gc_row, tot_row, beta_row):
    r = v.shape[0]
    rb = PREP_BLOCKS * BLK
    assert r % rb == 0 and BLK == HEAD_DIM
    ph = PREP_HEADS
    head_blk = pl.BlockSpec((rb, ph * HEAD_DIM), lambda h, t: (t, h))
    k_blk = pl.BlockSpec((rb, ph * HEAD_DIM), lambda h, t: (t, h + N_HEADS // ph))
    row_blk = pl.BlockSpec((2, ph, 1, rb), lambda h, t: (0, h, 0, t))
    out_blk = pl.BlockSpec((2, ph, rb, HEAD_DIM), lambda h, t: (0, h, t, 0))
    out_t_blk = pl.BlockSpec((2, ph, HEAD_DIM, rb), lambda h, t: (0, h, 0, t))
    sds = lambda shape, dt: jax.ShapeDtypeStruct((2, N_HEADS) + shape, dt)
    return pl.pallas_call(
        _dn_prep_kernel,
        grid=(N_HEADS // ph, r // rb),
        in_specs=[head_blk, k_blk, head_blk, row_blk, row_blk, row_blk],
        out_specs=[out_blk, out_blk, out_blk, out_t_blk, out_blk],
        out_shape=[sds((r, HEAD_DIM), F32), sds((r, HEAD_DIM), BF16), sds((r, HEAD_DIM), BF16),
                   sds((HEAD_DIM, r), BF16), sds((r, HEAD_DIM), BF16)],
        compiler_params=_cparams(("arbitrary", "arbitrary")),
        name="dn_prep",
    )(qk, qk, v, gc_row, tot_row, beta_row)


SCAN_ROWS = 256


SCAN_HEADS = 8


def _dn_scan_kernel(u0, u1, w0, w1, qd0, qd1, kdt0, kdt1, qk0, qk1, tot0, tot1, o0, o1, s_ref):
    @pl.when(pl.program_id(1) == 0)
    def _():
        s_ref[...] = jnp.zeros_like(s_ref)

    n_chunks = SCAN_ROWS // DN_CHUNK
    dirs = ((u0, w0, qd0, kdt0, qk0, tot0, o0), (u1, w1, qd1, kdt1, qk1, tot1, o1))
    chains = [(d, hh) for d in range(2) for hh in range(SCAN_HEADS)]
    s = {ch: s_ref[ch[0], ch[1]] for ch in chains}
    for step in range(n_chunks):
        cs = {0: step, 1: n_chunks - 1 - step}
        rs = {d: pl.ds(cs[d] * DN_CHUNK, DN_CHUNK) for d in range(2)}
        ws = {}
        for d, hh in chains:
            w_ref, qd_ref = dirs[d][1], dirs[d][2]
            wq = jnp.concatenate([w_ref[0, hh, rs[d], :], qd_ref[0, hh, rs[d], :]], axis=0)
            ws[d, hh] = _dot(wq, s[d, hh].astype(BF16))
        vnb = {(d, hh): (dirs[d][0][0, hh, rs[d], :] - ws[d, hh][:DN_CHUNK]).astype(BF16) for d, hh in chains}
        for d, hh in chains:
            kdt_ref, qk_ref, tot_ref, o_ref = dirs[d][3:]
            half = (cs[d] % 2) * DN_CHUNK
            qk = qk_ref[0, hh, rs[d], :][:, half:half + DN_CHUNK]
            o_ref[rs[d], hh * HEAD_DIM:(hh + 1) * HEAD_DIM] = ws[d, hh][DN_CHUNK:] + _dot(qk, vnb[d, hh])
            g_last = jnp.exp(tot_ref[0, hh, :, pl.ds(cs[d] * DN_CHUNK, 1)])
            s[d, hh] = s[d, hh] * g_last + _dot(kdt_ref[0, hh, :, rs[d]], vnb[d, hh])
    for d, hh in chains:
        s_ref[d, hh] = s[d, hh]


def _dn_scan(u, w, qd, kdt, qk, tot_row, *, n_lat):
    r = u.shape[2]
    nlb = n_lat // SCAN_ROWS
    assert r - n_lat == SCAN_ROWS and N_HEADS % SCAN_HEADS == 0
    rb = (lambda t: jnp.where(t == 0, nlb, t - 1), lambda t: jnp.where(t == 0, nlb, nlb - t))

    def pair(arr, shape, transposed=False):
        specs = []
        for d in range(2):
            if transposed:
                imap = lambda h, t, d=d: (d, h, 0, rb[d](t))
            else:
                imap = lambda h, t, d=d: (d, h, rb[d](t), 0)
            specs.append(pl.BlockSpec((1, SCAN_HEADS) + shape, imap))
        return specs, [arr, arr]

    in_specs, args = [], []
    for arr, shape, tr in ((u, (SCAN_ROWS, HEAD_DIM), False), (w, (SCAN_ROWS, HEAD_DIM), False),
                           (qd, (SCAN_ROWS, HEAD_DIM), False), (kdt, (HEAD_DIM, SCAN_ROWS), True),
                           (qk, (SCAN_ROWS, HEAD_DIM), False), (tot_row, (1, SCAN_ROWS), True)):
        sp, ar = pair(arr, shape, tr)
        in_specs += sp
        args += ar
    out_specs = [pl.BlockSpec((SCAN_ROWS, SCAN_HEADS * HEAD_DIM), lambda h, t, d=d: (rb[d](t), h)) for d in range(2)]
    return pl.pallas_call(
        _dn_scan_kernel,
        grid=(N_HEADS // SCAN_HEADS, nlb + 1),
        in_specs=in_specs,
        out_specs=out_specs,
        out_shape=[jax.ShapeDtypeStruct((r, DN_DIM), F32)] * 2,
        scratch_shapes=[pltpu.VMEM((2, SCAN_HEADS, HEAD_DIM, HEAD_DIM), F32)],
        compiler_params=_cparams(("arbitrary", "arbitrary")),
        name="dn_scan",
    )(*args)


def _dn_out_kernel(of_ref, ob_ref, gate_ref, ng_ref, y_ref):
    for h in range(N_HEADS):
        cs = slice(h * HEAD_DIM, (h + 1) * HEAD_DIM)
        o = of_ref[:, cs] + ob_ref[:, cs]
        o = o * lax.rsqrt(jnp.mean(o * o, axis=-1, keepdims=True) + 1e-6) * ng_ref[...]
        y_ref[:, cs] = (o * _silu(gate_ref[:, cs])).astype(y_ref.dtype)


def _dn_out(o_f, o_b, gate, norm_g):
    r, n = o_f.shape
    row = pl.BlockSpec((ELT_ROWS, n), lambda i: (i, 0))
    return pl.pallas_call(
        _dn_out_kernel,
        grid=(r // ELT_ROWS,),
        in_specs=[row, row, row, pl.BlockSpec((1, HEAD_DIM), lambda i: (0, 0))],
        out_specs=row,
        out_shape=jax.ShapeDtypeStruct((r, n), BF16),
        compiler_params=_cparams(("arbitrary",)),
        name="dn_out",
    )(o_f, o_b, gate, norm_g.astype(F32).reshape(1, HEAD_DIM))


S5_TG = LANES // S5_GROUP
S5_FOLD = S5_CHUNK * LANES
S5_SW = S5_TG * 2 * S5_STATE
S5_PAIR = 2 * SUBLANES


def _s5_tile_params(lam_re, lam_im, log_step, b_re, b_im, c_re, c_im):
    f = lambda a: a.astype(F32)
    lr, li, br, bi, cr, ci = map(f, (lam_re, lam_im, b_re, b_im, c_re, c_im))
    dt = jnp.exp(f(log_step))[..., None]
    mag, ang = jnp.exp(lr * dt), li * dt
    lbr, lbi = mag * jnp.cos(ang), mag * jnp.sin(ang)
    den = lr * lr + li * li
    zr = ((lbr - 1.0) * lr + lbi * li) / den
    zi = (lbi * lr - (lbr - 1.0) * li) / den
    bbr = zr[..., None] * br - zi[..., None] * bi
    bbi = zr[..., None] * bi + zi[..., None] * br
    g, p = lbr.shape[1], lbr.shape[2]
    nt = g // S5_TG
    tile_rows = lambda a: a.reshape(2, nt, LANES, p)
    per_row = lambda a: tile_rows(jnp.repeat(a, S5_GROUP, axis=1))
    cat = lambda a, b: jnp.concatenate([a, b], axis=-1)
    lrr = cat(per_row(lbr), per_row(lbr))
    lii = cat(-per_row(lbi), per_row(lbi))
    bbc = cat(tile_rows(jnp.swapaxes(bbr, 2, 3)), tile_rows(jnp.swapaxes(bbi, 2, 3)))
    cc = cat(tile_rows(cr), tile_rows(ci))

    def cmul(xr, xi, yr, yi):
        return xr * yr - xi * yi, xr * yi + xi * yr

    ar, ai = lbr, lbi
    for _ in range(int(math.log2(S5_CHUNK))):
        ar, ai = cmul(ar, ai, ar, ai)
    pw = [(ar, ai)]
    for _ in range(SUBLANES - 1):
        pw.append(cmul(*pw[-1], ar, ai))
    plane = lambda x: x.reshape(2, nt, S5_TG * p)
    rows = lambda xr, xi: jnp.stack([plane(xr), plane(xi)], axis=2)
    steps = jnp.stack([jnp.broadcast_to(rows(*pw[n - 1])[:, :, :, None, :], (2, nt, 2, SUBLANES, S5_TG * p))
                       for n in (1, 2, 4)], axis=2)
    fwd = jnp.stack([rows(*pw[j])[0] for j in range(SUBLANES)], axis=2)
    bwd = jnp.stack([rows(*pw[SUBLANES - 1 - j])[1] for j in range(SUBLANES)], axis=2)
    consts = jnp.concatenate([steps, jnp.stack([fwd, bwd])[:, :, None]], axis=2)
    return lrr, lii, bbc, cc, consts.transpose(1, 0, 2, 3, 4, 5)


def _cmul_lanes(x, rr, ii):
    return x * rr + pltpu.roll(x, S5_STATE, 1) * ii


def _s5_ops_kernel(lrr_ref, lii_ref, bbc_ref, cc_ref, *out_refs, toeplitz):
    row = lax.broadcasted_iota(jnp.int32, (LANES, LANES), 0)
    lane = lax.broadcasted_iota(jnp.int32, (LANES, LANES), 1)
    same_group = (row // S5_GROUP) == (lane // S5_GROUP)
    conj = jnp.where(lane < S5_STATE, 1.0, -1.0)
    c = S5_CHUNK
    lag = []
    for d in range(2):
        rr, ii = lrr_ref[d, 0], lii_ref[d, 0]
        xs, zs = [bbc_ref[d, 0]], [cc_ref[d, 0]]
        for _ in range(c):
            xs.append(_cmul_lanes(xs[-1], rr, ii))
            zs.append(_cmul_lanes(zs[-1], rr, ii))
        if toeplitz:
            zc = zs[0] * conj
            z1, z2 = _split2(zc)
            ks = []
            for l in range(c):
                x1, x2 = _split2(xs[l])
                k = (lax.dot_general(x1, z1, _NT, preferred_element_type=F32)
                     + lax.dot_general(x1, z2, _NT, preferred_element_type=F32)
                     + lax.dot_general(x2, z1, _NT, preferred_element_type=F32))
                ks.append(jnp.where(same_group, k, 0.0))
            lag.append(ks)
        else:
            bc_ref, cct_ref = out_refs
            for i in range(c):
                xin = xs[c - 1 - i] if d == 0 else xs[i]
                zout = (zs[i + 1] if d == 0 else zs[c - i]) * conj
                rs = slice(i * LANES, (i + 1) * LANES)
                low = lane < S5_STATE
                for src, dst in ((xin, bc_ref), (zout, cct_ref)):
                    swapped = pltpu.roll(src, S5_STATE, 1)
                    for t in range(S5_TG // 2):
                        even = (row // S5_GROUP) == 2 * t
                        odd = (row // S5_GROUP) == 2 * t + 1
                        re = jnp.where(low, jnp.where(even, src, 0.0), jnp.where(odd, swapped, 0.0))
                        im = jnp.where(low, jnp.where(even, swapped, 0.0), jnp.where(odd, src, 0.0))
                        dst[d, 0, rs, t * LANES:(t + 1) * LANES] = re.astype(BF16)
                        dst[d, 0, rs, S5_SW // 2 + t * LANES:S5_SW // 2 + (t + 1) * LANES] = im.astype(BF16)
    if toeplitz:
        (m_ref,) = out_refs
        fwd = [k.astype(BF16) for k in lag[0]]
        bwd = [k.astype(BF16) for k in lag[1]]
        diag = (lag[0][0] + lag[1][0]).astype(BF16)
        for i in range(c):
            for j in range(c):
                blk = diag if i == j else (fwd[j - i] if j > i else bwd[i - j])
                m_ref[0, i * LANES:(i + 1) * LANES, j * LANES:(j + 1) * LANES] = blk


def _s5_ops(lrr, lii, bbc, cc):
    nt = lrr.shape[1]
    par = pl.BlockSpec((2, 1, LANES, LANES), lambda q: (0, q, 0, 0))
    common = dict(grid=(nt,), in_specs=[par] * 4, compiler_params=_cparams(("arbitrary",)))
    m = pl.pallas_call(
        functools.partial(_s5_ops_kernel, toeplitz=True),
        out_specs=pl.BlockSpec((1, S5_FOLD, S5_FOLD), lambda q: (q, 0, 0)),
        out_shape=jax.ShapeDtypeStruct((nt, S5_FOLD, S5_FOLD), BF16),
        name="s5_ops_toeplitz", **common)(lrr, lii, bbc, cc)
    st = pl.BlockSpec((2, 1, S5_FOLD, S5_SW), lambda q: (0, q, 0, 0))
    bc, cct = pl.pallas_call(
        functools.partial(_s5_ops_kernel, toeplitz=False),
        out_specs=[st, st],
        out_shape=[jax.ShapeDtypeStruct((2, nt, S5_FOLD, S5_SW), BF16)] * 2,
        name="s5_ops_state", **common)(lrr, lii, bbc, cc)
    return m, bc, cct


def _s5_state_kernel(ut_ref, bc_ref, k_ref, e_ref, uf_ref, vf_ref, vb_ref, *, n_lat_c, n_ctx_c):
    n_all = n_lat_c + n_ctx_c
    for i in range(S5_CHUNK):
        uf_ref[0, :, i * LANES:(i + 1) * LANES] = ut_ref[0, pl.ds(i, n_all, stride=S5_CHUNK), :].astype(BF16)
    u = uf_ref[0]
    vf = _dot(u, bc_ref[0, 0])
    vf_ref[pl.ds(0, n_ctx_c), :] = vf[n_lat_c:]
    vf_ref[pl.ds(n_ctx_c, n_lat_c), :] = vf[:n_lat_c]
    vb_ref[...] = _dot(u, bc_ref[1, 0])
    sub = lax.broadcasted_iota(jnp.int32, (SUBLANES, LANES), 0)

    half = S5_SW // 2
    lanes = [(d, slice(g * LANES, (g + 1) * LANES), slice(half + g * LANES, half + (g + 1) * LANES))
             for d in range(2) for g in range(half // LANES)]

    def shift(x, d, sh):
        if d == 0:
            return jnp.where(sub >= sh, pltpu.roll(x, sh, 0), 0.0)
        return jnp.where(sub < SUBLANES - sh, pltpu.roll(x, SUBLANES - sh, 0), 0.0)

    def axpy(x, s, d, n, cs):
        ar, ai = k_ref[0, d, n, 0, :, cs], k_ref[0, d, n, 1, :, cs]
        return x[0] + (s[0] * ar - s[1] * ai), x[1] + (s[1] * ar + s[0] * ai)

    def tile_scans(xs, carry_rows):
        for n, sh in enumerate((1, 2, 4)):
            shifted = [(shift(x[0], d, sh), shift(x[1], d, sh)) for x, (d, _, _) in zip(xs, lanes)]
            xs = [axpy(x, s, d, n, cr) for x, s, (d, cr, _) in zip(xs, shifted, lanes)]
        cbs = [tuple(jnp.broadcast_to(c, (SUBLANES, LANES)) for c in cc) for cc in carry_rows]
        xs = [axpy(x, cb, d, 3, cr) for x, cb, (d, cr, _) in zip(xs, cbs, lanes)]
        out = []
        for x, cb, (d, _, _) in zip(xs, cbs, lanes):
            if d == 0:
                excl = tuple(jnp.where(sub == 0, c, pltpu.roll(v, 1, 0)) for v, c in zip(x, cb))
                out.append((excl, tuple(v[SUBLANES - 1:SUBLANES, :] for v in x)))
            else:
                excl = tuple(jnp.where(sub == SUBLANES - 1, c, pltpu.roll(v, SUBLANES - 1, 0)) for v, c in zip(x, cb))
                out.append((excl, tuple(v[0:1, :] for v in x)))
        return out

    n_iter = n_all // S5_PAIR
    src = (vf_ref, vb_ref)

    def body(t, carries):
        base = (pl.multiple_of(t * S5_PAIR, S5_PAIR), pl.multiple_of((n_iter - 1 - t) * S5_PAIR, S5_PAIR))
        load = lambda off: [tuple(src[d][pl.ds(base[d] + off(d) * SUBLANES, SUBLANES), cs] for cs in (cr, ci))
                            for d, cr, ci in lanes]
        first = tile_scans(load(lambda d: d), carries)
        second = tile_scans(load(lambda d: 1 - d), [c for _, c in first])
        for (d, cr, ci), (e1, _), (e2, _) in zip(lanes, first, second):
            lo, hi = (e1, e2) if d == 0 else (e2, e1)
            for part, cs in enumerate((cr, ci)):
                e_ref[d, 0, pl.ds(base[d], S5_PAIR), cs] = jnp.concatenate([lo[part], hi[part]], axis=0).astype(BF16)
        return tuple(c for _, c in second)

    zero = jnp.zeros((1, LANES), F32)
    lax.fori_loop(0, n_iter, body, tuple((zero, zero) for _ in lanes))


def _s5_out_kernel(uf_ref, m_ref, cct_ref, e_ref, y_ref, *, n_lat_c, n_ctx_c):
    y = _dot(uf_ref[0, pl.ds(0, n_lat_c), :], m_ref[0])
    y = y + lax.dot_general(e_ref[0, 0, pl.ds(n_ctx_c, n_lat_c), :], cct_ref[0, 0], _NT, preferred_element_type=F32)
    y = y + lax.dot_general(e_ref[1, 0, pl.ds(0, n_lat_c), :], cct_ref[1, 0], _NT, preferred_element_type=F32)
    first = pl.program_id(1) * (S5_SW // LANES)
    for ii in range(S5_SW // LANES):
        y_ref[0, pl.ds(first + ii, n_lat_c, stride=S5_CHUNK), :] = y[:, ii * LANES:(ii + 1) * LANES]


def _s5_scan(ut, m, bc, cct, consts, *, n_lat):
    nt, r, _ = ut.shape
    n_all = r // S5_CHUNK
    n_lat_c = n_lat // S5_CHUNK
    n_ctx_c = n_all - n_lat_c
    assert n_lat_c % S5_PAIR == 0 and n_ctx_c % S5_PAIR == 0
    dims = dict(n_lat_c=n_lat_c, n_ctx_c=n_ctx_c)
    e, uf = pl.pallas_call(
        functools.partial(_s5_state_kernel, **dims),
        grid=(nt,),
        in_specs=[
            pl.BlockSpec((1, r, LANES), lambda q: (q, 0, 0)),
            pl.BlockSpec((2, 1, S5_FOLD, S5_SW), lambda q: (0, q, 0, 0)),
            pl.BlockSpec((1, 2, 4, 2, SUBLANES, S5_SW // 2), lambda q: (q, 0, 0, 0, 0, 0)),
        ],
        out_specs=[pl.BlockSpec((2, 1, n_all, S5_SW), lambda q: (0, q, 0, 0)),
                   pl.BlockSpec((1, n_all, S5_FOLD), lambda q: (q, 0, 0))],
        out_shape=[jax.ShapeDtypeStruct((2, nt, n_all, S5_SW), BF16),
                   jax.ShapeDtypeStruct((nt, n_all, S5_FOLD), BF16)],
        scratch_shapes=[pltpu.VMEM((n_all, S5_SW), F32), pltpu.VMEM((n_all, S5_SW), F32)],
        compiler_params=_cparams(("arbitrary",)),
        name="s5_state",
    )(ut, bc, consts)
    halves = S5_FOLD // S5_SW
    return pl.pallas_call(
        functools.partial(_s5_out_kernel, **dims),
        grid=(nt, halves),
        in_specs=[
            pl.BlockSpec((1, n_all, S5_FOLD), lambda q, j: (q, 0, 0)),
            pl.BlockSpec((1, S5_FOLD, S5_SW), lambda q, j: (q, 0, j)),
            pl.BlockSpec((2, 1, S5_SW, S5_SW), lambda q, j: (0, q, j, 0)),
            pl.BlockSpec((2, 1, n_all, S5_SW), lambda q, j: (0, q, 0, 0)),
        ],
        out_specs=pl.BlockSpec((1, n_lat, LANES), lambda q, j: (q, 0, 0)),
        out_shape=jax.ShapeDtypeStruct((nt, n_lat, LANES), F32),
        compiler_params=_cparams(("arbitrary", "arbitrary")),
        name="s5_out",
    )(uf, m, cct, e)


def _gelu_skip_kernel(y_ref, u_ref, d_ref, o_ref):
    for q in range(y_ref.shape[0]):
        cs = slice(q * LANES, (q + 1) * LANES)
        z = y_ref[q] + d_ref[:, cs] * u_ref[q]
        inner = math.sqrt(2.0 / math.pi) * (z + 0.044715 * (z * z * z))
        o_ref[:, cs] = (0.5 * z * (1.0 + jnp.tanh(inner))).astype(o_ref.dtype)


def _gelu_skip(y, u, d_skip):
    nt, r, _ = y.shape
    d = nt * LANES
    tile = pl.BlockSpec((nt, ELT_ROWS, LANES), lambda i: (0, i, 0))
    row = pl.BlockSpec((ELT_ROWS, d), lambda i: (i, 0))
    return pl.pallas_call(
        _gelu_skip_kernel,
        grid=(r // ELT_ROWS,),
        in_specs=[tile, tile, pl.BlockSpec((1, d), lambda i: (0, 0))],
        out_specs=row,
        out_shape=jax.ShapeDtypeStruct((r, d), BF16),
        compiler_params=_cparams(("arbitrary",)),
        name="gelu_skip",
    )(y, u, d_skip.astype(F32).reshape(1, d))


def _mixer_na_gdn(u, w_in, w_out, rpb, conv_w, a_log, dt_bias, norm_g, *, n_lat):
    r = u.shape[0]
    n_main = 3 * NA_DIM + 4 * DN_DIM
    qkv_na = _matmul(u, w_in, col0=0, ncols=3 * NA_DIM, tn=256, out_dtype=BF16)
    qkv_dn = _matmul(u, w_in, col0=3 * NA_DIM, ncols=3 * DN_DIM, tn=256, out_dtype=F32)
    gate = _matmul(u, w_in, col0=3 * NA_DIM + 3 * DN_DIM, ncols=DN_DIM, tn=256, out_dtype=F32)
    w_ab = jnp.pad(w_in[:, n_main:], ((0, 0), (0, LANES - 4 * N_HEADS)))
    ab = _matmul(u, w_ab, col0=0, ncols=LANES, tn=LANES, out_dtype=F32)

    o_na = _na_attention(qkv_na, _na_bias_table(rpb), n_lat=n_lat)

    cos, sin = _rope_tables(n_lat)
    qk = _dn_conv(qkv_dn, conv_w.astype(F32), cos, sin, n_lat=n_lat, col0=0, ncols=2 * DN_DIM, qk=True)
    v = _dn_conv(qkv_dn, conv_w.astype(F32), cos, sin, n_lat=n_lat, col0=2 * DN_DIM, ncols=DN_DIM, qk=False)
    gb = _dn_gates(ab, a_log, dt_bias)
    gb_t = gb.T.reshape(4, 2, N_HEADS, 1, r)
    u_c, w_c, qd, kdt, qkm = _dn_prep(qk, v, gb_t[2], gb_t[3], gb_t[1])
    o_f, o_b = _dn_scan(u_c, w_c, qd, kdt, qkm, gb_t[3], n_lat=n_lat)
    o_dn = _dn_out(o_f, o_b, gate, norm_g)
    return _matmul_cat(o_na, o_dn, w_out, tn=512, out_dtype=BF16)


def _mixer_s5(ut, lam_re, lam_im, log_step, b_re, b_im, c_re, c_im, d_skip, w_out, w_gate, *, n_lat):
    d = ut.shape[0] * LANES
    lrr, lii, bbc, cc, consts = _s5_tile_params(lam_re, lam_im, log_step, b_re, b_im, c_re, c_im)
    m, bc, cct = _s5_ops(lrr, lii, bbc, cc)
    y = _s5_scan(ut, m, bc, cct, consts, n_lat=n_lat)
    gz = _gelu_skip(y, ut, d_skip)
    return _matmul_pair(gz, w_out, w_gate, col1=0, col2=0, ncols=d, tn=256, gate_second=True, out_dtype=BF16)


def kernel(x, c, ctx, c_ctx, w_mod, b_mod, ln_g, ln_b, ffn_w_in, ffn_w_out, ab_w_in, ab_w_out, na_rpb, dn_conv_w, dn_a_log, dn_dt_bias, dn_norm_g, s5_lam_re, s5_lam_im, s5_log_step, s5_b_re, s5_b_im, s5_c_re, s5_c_im, s5_d, s5_w_out, s5_w_gate):
    depth, d = w_mod.shape[0], x.shape[2]
    assert depth == 2 and x.shape[0] == 1, "layer 0 is the NA/DeltaNet layer, layer 1 the final S5 layer"
    n_lat = x.shape[1]
    alpha = (2.0 * depth) ** 0.25
    h = jnp.concatenate([x[0], ctx[0]], axis=0)
    mods = _modulation(c, c_ctx, w_mod, b_mod)
    lng = ln_g.astype(F32).reshape(depth * 3, 1, d)
    lnb = ln_b.astype(F32).reshape(depth * 3, 1, d)
    res_ln = functools.partial(_res_ln, mods=mods, ln_g=lng, ln_b=lnb, alpha=alpha, n_lat=n_lat)
    w_out_bf = ffn_w_out.astype(BF16)
    ffn = lambda uu, l, s: _ffn(uu, ffn_w_in, w_out_bf, (l, s))

    u = _modulate(h, mods, layer=0, sub=0, n_lat=n_lat)
    h, u = res_ln(h, ffn(u, 0, 0), layer=0, sub=0, weight=MACARON_WEIGHT, next_mod=(0, 1))
    y = _mixer_na_gdn(u, ab_w_in[0], ab_w_out[0], na_rpb[0], dn_conv_w[0], dn_a_log[0], dn_dt_bias[0], dn_norm_g[0],
                      n_lat=n_lat)
    h, u = res_ln(h, y, layer=0, sub=1, weight=1.0, next_mod=(0, 2))
    h, u = res_ln(h, ffn(u, 0, 1), layer=0, sub=2, weight=MACARON_WEIGHT, next_mod=(1, 0))

    h, ut = res_ln(h, ffn(u, 1, 0), layer=1, sub=0, weight=MACARON_WEIGHT, next_mod=(1, 1), u_dtypes=(F32,), u_tiles=True)
    y = _mixer_s5(ut, s5_lam_re[0], s5_lam_im[0], s5_log_step[0], s5_b_re[0], s5_b_im[0], s5_c_re[0], s5_c_im[0],
                  s5_d[0], s5_w_out[0], s5_w_gate[0], n_lat=n_lat)
    h, u = res_ln(h, y, layer=1, sub=1, weight=1.0, next_mod=(1, 2))
    (out,) = res_ln(h, ffn(u, 1, 1), layer=1, sub=2, weight=MACARON_WEIGHT, next_mod=None)
    return out[None]
```
